```python
import jax, jax.numpy as jnp
from jax import lax
import numpy as np

D_MODEL = 1024
BATCH = 4
SEQ = 4096
DEPTH = 4

CHUNK = 64
EPS = 1e-6
N_A = DEPTH // 2
N_B = DEPTH - N_A
A_CHUNK = 128
A_DFF = 2 * D_MODEL
A_GROUPS = 8
A_GROUP_DIM = A_DFF // A_GROUPS
B_HEADS = 16
B_HEAD_DIM = D_MODEL // B_HEADS
N_LEFT_CHUNKS = 8
BAND = (N_LEFT_CHUNKS + 1) * CHUNK
MAX_REL = 256
ATTN_SCALE = B_HEAD_DIM ** -0.5
NEG_INF = -1e30
FFN_HIDDEN = -(-8 * D_MODEL // (3 * 256)) * 256

kernel_name = "yoco_gmlp_chunked_relbias_attention_trunk"


def _rms_norm(x, g):
    xf = x.astype(jnp.float32)
    y = xf * lax.rsqrt(jnp.mean(xf * xf, axis=-1, keepdims=True) + EPS)
    return (y * g.astype(jnp.float32)).astype(x.dtype)


def _spatial_mask():
    pos = jnp.arange(A_CHUNK) // CHUNK
    return pos[:, None] >= pos[None, :]


def _gmlp_mixer(x, g_norm, w_in, g_sgu, w_s, b_s, w_out):
    b, s, _ = x.shape
    h = _rms_norm(x, g_norm)
    z = jax.nn.gelu(h @ w_in, approximate=False)
    u, v = jnp.split(z, 2, axis=-1)
    v = _rms_norm(v, g_sgu)
    v = v.reshape(b, s // A_CHUNK, A_CHUNK, A_GROUPS, A_GROUP_DIM)
    w = w_s * _spatial_mask().astype(w_s.dtype)[None]
    v = jnp.einsum('gij,bnjgc->bnigc', w, v) + b_s.T[None, None, :, :, None]
    v = v.reshape(b, s, A_DFF)
    return (u * v) @ w_out


def _swiglu_ffn(x, g_norm, w_gate_up, w_down):
    h = _rms_norm(x, g_norm)
    gate, up = jnp.split(h @ w_gate_up, 2, axis=-1)
    return (jax.nn.silu(gate) * up) @ w_down


def _shared_kv(x, g_norm, w_kv):
    b, s, _ = x.shape
    h = _rms_norm(x, g_norm)
    k, v = jnp.split(h @ w_kv, 2, axis=-1)
    k = k.reshape(b, s, B_HEADS, B_HEAD_DIM)
    v = v.reshape(b, s, B_HEADS, B_HEAD_DIM)
    pad = ((0, 0), (N_LEFT_CHUNKS * CHUNK, 0), (0, 0), (0, 0))
    return jnp.pad(k, pad), jnp.pad(v, pad)


def _rel_index():
    qi = jnp.arange(CHUNK)[:, None]
    kj = jnp.arange(BAND)[None, :] - N_LEFT_CHUNKS * CHUNK
    return jnp.clip(qi - kj, -MAX_REL, MAX_REL) + MAX_REL


def _chunked_relbias_attention(x, g_norm, w_q, rel_table, w_o, k_pad, v_pad):
    b, s, _ = x.shape
    n_chunks = s // CHUNK
    q = (_rms_norm(x, g_norm) @ w_q).reshape(b, n_chunks, CHUNK, B_HEADS, B_HEAD_DIM)
    q = jnp.moveaxis(q, 1, 0)
    bias = rel_table[:, _rel_index()].astype(jnp.float32)
    key_offset = jnp.arange(BAND) - N_LEFT_CHUNKS * CHUNK

    def attend(args):
        c, q_c = args
        k_c = lax.dynamic_slice_in_dim(k_pad, c * CHUNK, BAND, axis=1)
        v_c = lax.dynamic_slice_in_dim(v_pad, c * CHUNK, BAND, axis=1)
        sc = jnp.einsum('bqhd,bkhd->bhqk', q_c, k_c).astype(jnp.float32) * ATTN_SCALE + bias
        valid = (c * CHUNK + key_offset) >= 0
        sc = jnp.where(valid, sc, NEG_INF)
        p = jax.nn.softmax(sc, axis=-1).astype(v_c.dtype)
        return jnp.einsum('bhqk,bkhd->bqhd', p, v_c)

    o = lax.map(attend, (jnp.arange(n_chunks), q))
    o = jnp.moveaxis(o, 0, 1).reshape(b, s, D_MODEL)
    return o @ w_o


def _normal(key, shape, scale):
    return jax.random.normal(key, shape, jnp.float32) * scale


def setup_inputs(seed: int = 0) -> dict:
    key = jax.random.key(seed)
    ks = jax.random.split(key, 18)
    return {
        "x": _normal(ks[0], (BATCH, SEQ, D_MODEL), 1.0),
        "a_norm": 1.0 + _normal(ks[1], (N_A, D_MODEL), 0.02),
        "a_w_in": _normal(ks[2], (N_A, D_MODEL, 2 * A_DFF), D_MODEL ** -0.5),
        "a_sgu_norm": 1.0 + _normal(ks[3], (N_A, A_DFF), 0.02),
        "a_w_spatial": _normal(ks[4], (N_A, A_GROUPS, A_CHUNK, A_CHUNK), A_CHUNK ** -0.5),
        "a_b_spatial": 1.0 + _normal(ks[5], (N_A, A_GROUPS, A_CHUNK), 0.1),
        "a_w_out": _normal(ks[6], (N_A, A_DFF, D_MODEL), A_DFF ** -0.5),
        "kv_norm": 1.0 + _normal(ks[7], (D_MODEL,), 0.02),
        "w_kv": _normal(ks[8], (D_MODEL, 2 * D_MODEL), D_MODEL ** -0.5),
        "b_norm": 1.0 + _normal(ks[9], (N_B, D_MODEL), 0.02),
        "b_w_q": _normal(ks[10], (N_B, D_MODEL, D_MODEL), D_MODEL ** -0.5),
        "b_rel_bias": _normal(ks[11], (N_B, B_HEADS, 2 * MAX_REL + 1), 0.1),
        "b_w_o": _normal(ks[12], (N_B, D_MODEL, D_MODEL), D_MODEL ** -0.5),
        "ffn_norm": 1.0 + _normal(ks[13], (DEPTH, D_MODEL), 0.02),
        "ffn_w_gate_up": _normal(ks[14], (DEPTH, D_MODEL, 2 * FFN_HIDDEN), D_MODEL ** -0.5),
        "ffn_w_down": _normal(ks[15], (DEPTH, FFN_HIDDEN, D_MODEL), FFN_HIDDEN ** -0.5),
        "final_norm": 1.0 + _normal(ks[16], (D_MODEL,), 0.02),
    }


def reference(x, a_norm, a_w_in, a_sgu_norm, a_w_spatial, a_b_spatial, a_w_out,
              kv_norm, w_kv, b_norm, b_w_q, b_rel_bias, b_w_o,
              ffn_norm, ffn_w_gate_up, ffn_w_down, final_norm):
    k_pad = v_pad = None
    for layer in range(DEPTH):
        if layer < N_A:
            i = layer
            x = x + _gmlp_mixer(x, a_norm[i], a_w_in[i], a_sgu_norm[i],
                                a_w_spatial[i], a_b_spatial[i], a_w_out[i])
        else:
            if layer == N_A:
                k_pad, v_pad = _shared_kv(x, kv_norm, w_kv)
            i = layer - N_A
            x = x + _chunked_relbias_attention(x, b_norm[i], b_w_q[i], b_rel_bias[i],
                                               b_w_o[i], k_pad, v_pad)
        x = x + _swiglu_ffn(x, ffn_norm[layer], ffn_w_gate_up[layer], ffn_w_down[layer])
    return _rms_norm(x, final_norm)
```

```python
import functools
import math

import jax
import jax.numpy as jnp
from jax import lax
from jax.experimental import pallas as pl
from jax.experimental.pallas import tpu as pltpu

EPS = 1e-6
CHUNK = 64
A_CHUNK = 128
N_LEFT_CHUNKS = 8
MAX_REL = 256
NEG_INF = -1e30

V7X_LANES = 128
V7X_MXU_DIM = 256
V7X_VMEM_LIMIT_BYTES = 56 * 1024 * 1024

TOKEN_TILE = 512
Q_GROUP_CHUNKS = 2
ATTN_TILE = 512

_BF16 = jnp.bfloat16
_F32 = jnp.float32


def _const_spec(shape):
    nd = len(shape)
    return pl.BlockSpec(shape, lambda *_: (0,) * nd, pipeline_mode=pl.Buffered(1))


def _rms_norm(x, g):
    return (x * lax.rsqrt(jnp.mean(x * x, axis=-1, keepdims=True) + EPS)) * g


def _dot(a, b):
    return jnp.dot(a, b, preferred_element_type=_F32)


def _gelu(x):
    return 0.5 * x * (1.0 + lax.erf(x * (1.0 / math.sqrt(2.0))))


def _gmlp_kernel(x_ref, gn_ref, win_ref, gsgu_ref, ws_ref, bs_ref, wout_ref,
                 o_ref, v_scr, t_scr, *, n_groups, gdim):
    tm = x_ref.shape[0]
    dff = n_groups * gdim
    x = x_ref[...]
    h = _rms_norm(x, gn_ref[...]).astype(_BF16)

    ss = jnp.zeros((tm, 1), _F32)
    for g in range(n_groups):
        z = _gelu(_dot(h, win_ref[:, dff + g * gdim: dff + (g + 1) * gdim]))
        v_scr[:, g * gdim:(g + 1) * gdim] = z
        ss = ss + jnp.sum(z * z, axis=-1, keepdims=True)
    r = lax.rsqrt(ss * (1.0 / dff) + EPS)

    row = lax.broadcasted_iota(jnp.int32, (A_CHUNK, A_CHUNK), 0) // CHUNK
    col = lax.broadcasted_iota(jnp.int32, (A_CHUNK, A_CHUNK), 1) // CHUNK
    causal = row >= col

    for g in range(n_groups):
        cols = slice(g * gdim, (g + 1) * gdim)
        vn = ((v_scr[:, cols] * r) * gsgu_ref[:, cols]).astype(_BF16)
        w = jnp.where(causal, ws_ref[g], 0.0).astype(_BF16)
        sp = jnp.concatenate(
            [_dot(w, vn[k * A_CHUNK:(k + 1) * A_CHUNK]) for k in range(tm // A_CHUNK)], axis=0)
        sp = sp + bs_ref[:, g:g + 1]
        u = _gelu(_dot(h, win_ref[:, cols]))
        t_scr[:, cols] = (u * sp).astype(_BF16)

    o_ref[...] = x + _dot(t_scr[...], wout_ref[...])


def _gmlp_layer(x2, g_norm, w_in, g_sgu, w_s, b_s, w_out):
    t, d = x2.shape
    n_groups = w_s.shape[0]
    dff = w_out.shape[0]
    gdim = dff // n_groups
    tm = TOKEN_TILE
    bs_tile = jnp.tile(b_s.T, (tm // A_CHUNK, 1))
    kern = functools.partial(_gmlp_kernel, n_groups=n_groups, gdim=gdim)
    return pl.pallas_call(
        kern,
        grid=(t // tm,),
        in_specs=[
            pl.BlockSpec((tm, d), lambda i: (i, 0)),
            _const_spec((1, d)),
            _const_spec((d, 2 * dff)),
            _const_spec((1, dff)),
            _const_spec(w_s.shape),
            _const_spec((tm, n_groups)),
            _const_spec((dff, d)),
        ],
        out_specs=pl.BlockSpec((tm, d), lambda i: (i, 0)),
        out_shape=jax.ShapeDtypeStruct((t, d), _F32),
        scratch_shapes=[pltpu.VMEM((tm, dff), _F32), pltpu.VMEM((tm, dff), _BF16)],
        compiler_params=pltpu.CompilerParams(
            dimension_semantics=("arbitrary",), vmem_limit_bytes=V7X_VMEM_LIMIT_BYTES),
        name="gmlp_mixer",
    )(x2, g_norm.reshape(1, d), w_in.astype(_BF16), g_sgu.reshape(1, dff), w_s, bs_tile,
      w_out.astype(_BF16))


def _ffn_kernel(x_ref, gn_ref, wgu_ref, wd_ref, fn_ref, o_ref, a_scr, *, hidden, final_norm):
    x = x_ref[...]
    h = _rms_norm(x, gn_ref[...]).astype(_BF16)
    ck = V7X_MXU_DIM
    for c in range(hidden // ck):
        gate = _dot(h, wgu_ref[:, c * ck:(c + 1) * ck])
        up = _dot(h, wgu_ref[:, hidden + c * ck: hidden + (c + 1) * ck])
        a_scr[:, c * ck:(c + 1) * ck] = ((gate * jax.nn.sigmoid(gate)) * up).astype(_BF16)
    y = x + _dot(a_scr[...], wd_ref[...])
    if final_norm:
        y = _rms_norm(y, fn_ref[...])
    o_ref[...] = y


def _ffn_layer(x2, g_norm, w_gate_up, w_down, g_final, final_norm):
    t, d = x2.shape
    hidden = w_down.shape[0]
    assert hidden % V7X_MXU_DIM == 0
    tm = TOKEN_TILE
    kern = functools.partial(_ffn_kernel, hidden=hidden, final_norm=final_norm)
    return pl.pallas_call(
        kern,
        grid=(t // tm,),
        in_specs=[
            pl.BlockSpec((tm, d), lambda i: (i, 0)),
            _const_spec((1, d)),
            _const_spec((d, 2 * hidden)),
            _const_spec((hidden, d)),
            _const_spec((1, d)),
        ],
        out_specs=pl.BlockSpec((tm, d), lambda i: (i, 0)),
        out_shape=jax.ShapeDtypeStruct((t, d), _F32),
        scratch_shapes=[pltpu.VMEM((tm, hidden), _BF16)],
        compiler_params=pltpu.CompilerParams(
            dimension_semantics=("arbitrary",), vmem_limit_bytes=V7X_VMEM_LIMIT_BYTES),
        name="swiglu_ffn",
    )(x2, g_norm.reshape(1, d), w_gate_up.astype(_BF16), w_down.astype(_BF16),
      g_final.reshape(1, d))


def _kv_kernel(x_ref, gn_ref, w_ref, k_ref, v_ref):
    j = pl.program_id(1)
    d = x_ref.shape[-1]

    @pl.when(j == 0)
    def _():
        k_ref[...] = jnp.zeros_like(k_ref)
        v_ref[...] = jnp.zeros_like(v_ref)

    @pl.when(j > 0)
    def _():
        h = _rms_norm(x_ref[0], gn_ref[...]).astype(_BF16)
        kv = _dot(h, w_ref[...])
        k_ref[0] = kv[:, :d].astype(_BF16)
        v_ref[0] = kv[:, d:].astype(_BF16)


def _shared_kv(x3, g_norm, w_kv):
    b, s, d = x3.shape
    pad = N_LEFT_CHUNKS * CHUNK
    tk = pad
    assert s % tk == 0
    out = jax.ShapeDtypeStruct((b, s + pad, d), _BF16)
    return pl.pallas_call(
        _kv_kernel,
        grid=(b, 1 + s // tk),
        in_specs=[
            pl.BlockSpec((1, tk, d), lambda bi, j: (bi, jnp.maximum(j - 1, 0), 0)),
            _const_spec((1, d)),
            _const_spec((d, 2 * d)),
        ],
        out_specs=[pl.BlockSpec((1, tk, d), lambda bi, j: (bi, j, 0))] * 2,
        out_shape=[out, out],
        compiler_params=pltpu.CompilerParams(
            dimension_semantics=("arbitrary", "arbitrary"),
            vmem_limit_bytes=V7X_VMEM_LIMIT_BYTES),
        name="shared_kv",
    )(x3, g_norm.reshape(1, d), w_kv.astype(_BF16))


def _bias_kernel(r_ref, o_ref):
    _, rows, width = o_ref.shape
    qc = lax.broadcasted_iota(jnp.int32, (1, width), 1) // CHUNK
    for r in range(rows):
        j = r // CHUNK
        in_band = (qc >= j) & (qc <= j + N_LEFT_CHUNKS)
        off = rows - 1 - r
        o_ref[:, r, :] = jnp.where(in_band, r_ref[:, off:off + width], NEG_INF)


def _group_bias(rel_table, group_chunks):
    heads = rel_table.shape[0]
    rows = group_chunks * CHUNK
    width = (group_chunks + N_LEFT_CHUNKS) * CHUNK
    dist = (rows - 1 + N_LEFT_CHUNKS * CHUNK) - jnp.arange(rows - 1 + width)
    rev = rel_table[:, jnp.clip(dist, -MAX_REL, MAX_REL) + MAX_REL]
    return pl.pallas_call(
        _bias_kernel,
        out_shape=jax.ShapeDtypeStruct((heads, rows, width), _F32),
        name="rel_bias",
    )(rev)


def _attn_kernel(x_ref, gn_ref, wq_ref, k_ref, v_ref, bias_ref, wo_ref, o_ref, q_scr, a_scr,
                 *, heads, head_dim, group_chunks):
    tq = x_ref.shape[1]
    rows = group_chunks * CHUNK
    width = (group_chunks + N_LEFT_CHUNKS) * CHUNK
    pair = V7X_LANES // head_dim
    scale = head_dim ** -0.5
    i = pl.program_id(1)

    x = x_ref[0]
    h = _rms_norm(x, gn_ref[...]).astype(_BF16)
    q_scr[...] = (_dot(h, wq_ref[...]) * scale).astype(_BF16)

    lane = lax.broadcasted_iota(jnp.int32, (1, V7X_LANES), 1)
    kcol = lax.broadcasted_iota(jnp.int32, (1, width), 1)

    def group_body(gi, carry):
        r0 = pl.multiple_of(gi * rows, rows)
        k0 = pl.multiple_of(i * tq + gi * rows, rows)
        valid = (k0 + kcol) >= N_LEFT_CHUNKS * CHUNK
        for hp in range(heads // pair):
            lanes = slice(hp * V7X_LANES, (hp + 1) * V7X_LANES)
            q2 = q_scr[pl.ds(r0, rows), lanes]
            k2 = k_ref[0, pl.ds(k0, width), lanes]
            v2 = v_ref[0, pl.ds(k0, width), lanes]
            o2 = jnp.zeros((rows, V7X_LANES), _F32)
            for sub in range(pair):
                mine = (lane >= sub * head_dim) & (lane < (sub + 1) * head_dim)
                qm = jnp.where(mine, q2, jnp.zeros_like(q2))
                s = lax.dot_general(qm, k2, (((1,), (1,)), ((), ())),
                                    preferred_element_type=_F32)
                s = s + bias_ref[hp * pair + sub]
                s = jnp.where(valid, s, NEG_INF)
                m = jnp.max(s, axis=-1, keepdims=True)
                p = jnp.exp(s - m)
                l = jnp.sum(p, axis=-1, keepdims=True)
                o = _dot(p.astype(_BF16), v2) / l
                o2 = jnp.where(mine, o, o2)
            a_scr[pl.ds(r0, rows), lanes] = o2.astype(_BF16)
        return carry

    lax.fori_loop(0, tq // rows, group_body, 0)
    o_ref[0] = x + _dot(a_scr[...], wo_ref[...])


def _attn_layer(x3, g_norm, w_q, bias, w_o, k_pad, v_pad, heads):
    b, s, d = x3.shape
    head_dim = d // heads
    tq = ATTN_TILE
    sp = k_pad.shape[1]
    kern = functools.partial(_attn_kernel, heads=heads, head_dim=head_dim,
                             group_chunks=Q_GROUP_CHUNKS)
    kv_spec = pl.BlockSpec((1, sp, d), lambda bi, i: (bi, 0, 0), pipeline_mode=pl.Buffered(1))
    return pl.pallas_call(
        kern,
        grid=(b, s // tq),
        in_specs=[
            pl.BlockSpec((1, tq, d), lambda bi, i: (bi, i, 0)),
            _const_spec((1, d)),
            _const_spec((d, d)),
            kv_spec,
            kv_spec,
            _const_spec(bias.shape),
            _const_spec((d, d)),
        ],
        out_specs=pl.BlockSpec((1, tq, d), lambda bi, i: (bi, i, 0)),
        out_shape=jax.ShapeDtypeStruct((b, s, d), _F32),
        scratch_shapes=[pltpu.VMEM((tq, d), _BF16), pltpu.VMEM((tq, d), _BF16)],
        compiler_params=pltpu.CompilerParams(
            dimension_semantics=("arbitrary", "arbitrary"),
            vmem_limit_bytes=V7X_VMEM_LIMIT_BYTES),
        name="banded_attention",
    )(x3, g_norm.reshape(1, d), w_q.astype(_BF16), k_pad, v_pad, bias, w_o.astype(_BF16))


def kernel(x, a_norm, a_w_in, a_sgu_norm, a_w_spatial, a_b_spatial, a_w_out, kv_norm, w_kv,
           b_norm, b_w_q, b_rel_bias, b_w_o, ffn_norm, ffn_w_gate_up, ffn_w_down, final_norm):
    b, s, d = x.shape
    n_a = a_norm.shape[0]
    depth = ffn_norm.shape[0]
    heads = b_rel_bias.shape[1]
    assert s % ATTN_TILE == 0 and (b * s) % TOKEN_TILE == 0
    assert ATTN_TILE % (Q_GROUP_CHUNKS * CHUNK) == 0 and TOKEN_TILE % A_CHUNK == 0

    x2 = x.reshape(b * s, d)
    k_pad = v_pad = None
    for layer in range(depth):
        if layer < n_a:
            x2 = _gmlp_layer(x2, a_norm[layer], a_w_in[layer], a_sgu_norm[layer],
                             a_w_spatial[layer], a_b_spatial[layer], a_w_out[layer])
        else:
            x3 = x2.reshape(b, s, d)
            if layer == n_a:
                k_pad, v_pad = _shared_kv(x3, kv_norm, w_kv)
            li = layer - n_a
            bias = _group_bias(b_rel_bias[li], Q_GROUP_CHUNKS)
            x2 = _attn_layer(x3, b_norm[li], b_w_q[li], bias, b_w_o[li], k_pad, v_pad,
                             heads).reshape(b * s, d)
        x2 = _ffn_layer(x2, ffn_norm[layer], ffn_w_gate_up[layer], ffn_w_down[layer],
                        final_norm, final_norm=(layer == depth - 1))
    return x2.reshape(b, s, d)
```

```python
import functools
import math

import jax
import jax.numpy as jnp
from jax import lax
from jax.experimental import pallas as pl
from jax.experimental.pallas import tpu as pltpu

EPS = 1e-6
CHUNK = 64
A_CHUNK = 128
N_LEFT_CHUNKS = 8
MAX_REL = 256
NEG_INF = -1e30
LOG2E = math.log2(math.e)

V7X_LANES = 128
V7X_MXU_DIM = 256
V7X_VMEM_LIMIT_BYTES = 56 * 1024 * 1024

TOKEN_TILE = 512
Q_GROUP_CHUNKS = 2
ATTN_TILE = 512
ATTN_SLOTS = 4

_BF16 = jnp.bfloat16
_F32 = jnp.float32


def _const_spec(shape):
    nd = len(shape)
    return pl.BlockSpec(shape, lambda *_: (0,) * nd, pipeline_mode=pl.Buffered(1))


def _rms_norm(x, g):
    return (x * lax.rsqrt(jnp.mean(x * x, axis=-1, keepdims=True) + EPS)) * g


def _dot(a, b):
    return jnp.dot(a, b, preferred_element_type=_F32)


def _gelu(x):
    return 0.5 * x * (1.0 + lax.erf(x * (1.0 / math.sqrt(2.0))))


def _gmlp_kernel(x_ref, gn_ref, win_ref, gsgu_ref, ws_ref, bs_ref, wout_ref,
                 o_ref, v_scr, t_scr, *, n_groups, gdim):
    tm = x_ref.shape[0]
    dff = n_groups * gdim
    x = x_ref[...]
    h = _rms_norm(x, gn_ref[...]).astype(_BF16)

    ss = jnp.zeros((tm, 1), _F32)
    for g in range(n_groups):
        z = _gelu(_dot(h, win_ref[:, dff + g * gdim: dff + (g + 1) * gdim]))
        v_scr[:, g * gdim:(g + 1) * gdim] = z
        ss = ss + jnp.sum(z * z, axis=-1, keepdims=True)
    r = lax.rsqrt(ss * (1.0 / dff) + EPS)

    row = lax.broadcasted_iota(jnp.int32, (A_CHUNK, A_CHUNK), 0) // CHUNK
    col = lax.broadcasted_iota(jnp.int32, (A_CHUNK, A_CHUNK), 1) // CHUNK
    causal = row >= col

    for g in range(n_groups):
        cols = slice(g * gdim, (g + 1) * gdim)
        vn = ((v_scr[:, cols] * r) * gsgu_ref[:, cols]).astype(_BF16)
        w = jnp.where(causal, ws_ref[g], 0.0).astype(_BF16)
        sp = jnp.concatenate(
            [_dot(w, vn[k * A_CHUNK:(k + 1) * A_CHUNK]) for k in range(tm // A_CHUNK)], axis=0)
        sp = sp + bs_ref[:, g:g + 1]
        u = _gelu(_dot(h, win_ref[:, cols]))
        t_scr[:, cols] = (u * sp).astype(_BF16)

    o_ref[...] = x + _dot(t_scr[...], wout_ref[...])


def _gmlp_layer(x2, g_norm, w_in, g_sgu, w_s, b_s, w_out):
    t, d = x2.shape
    n_groups = w_s.shape[0]
    dff = w_out.shape[0]
    gdim = dff // n_groups
    tm = TOKEN_TILE
    bs_tile = jnp.tile(b_s.T, (tm // A_CHUNK, 1))
    kern = functools.partial(_gmlp_kernel, n_groups=n_groups, gdim=gdim)
    return pl.pallas_call(
        kern,
        grid=(t // tm,),
        in_specs=[
            pl.BlockSpec((tm, d), lambda i: (i, 0)),
            _const_spec((1, d)),
            _const_spec((d, 2 * dff)),
            _const_spec((1, dff)),
            _const_spec(w_s.shape),
            _const_spec((tm, n_groups)),
            _const_spec((dff, d)),
        ],
        out_specs=pl.BlockSpec((tm, d), lambda i: (i, 0)),
        out_shape=jax.ShapeDtypeStruct((t, d), _F32),
        scratch_shapes=[pltpu.VMEM((tm, dff), _F32), pltpu.VMEM((tm, dff), _BF16)],
        compiler_params=pltpu.CompilerParams(
            dimension_semantics=("arbitrary",), vmem_limit_bytes=V7X_VMEM_LIMIT_BYTES),
        name="gmlp_mixer",
    )(x2, g_norm.reshape(1, d), w_in.astype(_BF16), g_sgu.reshape(1, dff), w_s, bs_tile,
      w_out.astype(_BF16))


def _ffn_kernel(x_ref, gn_ref, wgu_ref, wd_ref, fn_ref, o_ref, a_scr, *, hidden, final_norm):
    x = x_ref[...]
    h = _rms_norm(x, gn_ref[...]).astype(_BF16)
    ck = V7X_MXU_DIM
    for c in range(hidden // ck):
        gate = _dot(h, wgu_ref[:, c * ck:(c + 1) * ck])
        up = _dot(h, wgu_ref[:, hidden + c * ck: hidden + (c + 1) * ck])
        a_scr[:, c * ck:(c + 1) * ck] = ((gate * jax.nn.sigmoid(gate)) * up).astype(_BF16)
    y = x + _dot(a_scr[...], wd_ref[...])
    if final_norm:
        y = _rms_norm(y, fn_ref[...])
    o_ref[...] = y


def _ffn_layer(x2, g_norm, w_gate_up, w_down, g_final, final_norm):
    t, d = x2.shape
    hidden = w_down.shape[0]
    assert hidden % V7X_MXU_DIM == 0
    tm = TOKEN_TILE
    kern = functools.partial(_ffn_kernel, hidden=hidden, final_norm=final_norm)
    return pl.pallas_call(
        kern,
        grid=(t // tm,),
        in_specs=[
            pl.BlockSpec((tm, d), lambda i: (i, 0)),
            _const_spec((1, d)),
            _const_spec((d, 2 * hidden)),
            _const_spec((hidden, d)),
            _const_spec((1, d)),
        ],
        out_specs=pl.BlockSpec((tm, d), lambda i: (i, 0)),
        out_shape=jax.ShapeDtypeStruct((t, d), _F32),
        scratch_shapes=[pltpu.VMEM((tm, hidden), _BF16)],
        compiler_params=pltpu.CompilerParams(
            dimension_semantics=("arbitrary",), vmem_limit_bytes=V7X_VMEM_LIMIT_BYTES),
        name="swiglu_ffn",
    )(x2, g_norm.reshape(1, d), w_gate_up.astype(_BF16), w_down.astype(_BF16),
      g_final.reshape(1, d))


def _kv_kernel(x_ref, gn_ref, w_ref, k_ref, v_ref):
    j = pl.program_id(1)
    d = x_ref.shape[-1]

    @pl.when(j == 0)
    def _():
        k_ref[...] = jnp.zeros_like(k_ref)
        v_ref[...] = jnp.zeros_like(v_ref)

    @pl.when(j > 0)
    def _():
        h = _rms_norm(x_ref[0], gn_ref[...]).astype(_BF16)
        kv = _dot(h, w_ref[...])
        k_ref[0] = kv[:, :d].astype(_BF16)
        v_ref[0] = kv[:, d:].astype(_BF16)


def _shared_kv(x3, g_norm, w_kv):
    b, s, d = x3.shape
    pad = N_LEFT_CHUNKS * CHUNK
    tk = pad
    assert s % tk == 0
    out = jax.ShapeDtypeStruct((b, s + pad, d), _BF16)
    return pl.pallas_call(
        _kv_kernel,
        grid=(b, 1 + s // tk),
        in_specs=[
            pl.BlockSpec((1, tk, d), lambda bi, j: (bi, jnp.maximum(j - 1, 0), 0)),
            _const_spec((1, d)),
            _const_spec((d, 2 * d)),
        ],
        out_specs=[pl.BlockSpec((1, tk, d), lambda bi, j: (bi, j, 0))] * 2,
        out_shape=[out, out],
        compiler_params=pltpu.CompilerParams(
            dimension_semantics=("arbitrary", "arbitrary"),
            vmem_limit_bytes=V7X_VMEM_LIMIT_BYTES),
        name="shared_kv",
    )(x3, g_norm.reshape(1, d), w_kv.astype(_BF16))


def _bias_kernel(r_ref, o_ref):
    _, rows, width = o_ref.shape
    qc = lax.broadcasted_iota(jnp.int32, (1, width), 1) // CHUNK
    for r in range(rows):
        j = r // CHUNK
        in_band = (qc >= j) & (qc <= j + N_LEFT_CHUNKS)
        off = rows - 1 - r
        o_ref[:, r, :] = jnp.where(in_band, r_ref[:, off:off + width] * LOG2E, NEG_INF)


def _group_bias(rel_table, group_chunks):
    heads = rel_table.shape[0]
    rows = group_chunks * CHUNK
    width = (group_chunks + N_LEFT_CHUNKS) * CHUNK
    dist = (rows - 1 + N_LEFT_CHUNKS * CHUNK) - jnp.arange(rows - 1 + width)
    rev = rel_table[:, jnp.clip(dist, -MAX_REL, MAX_REL) + MAX_REL]
    return pl.pallas_call(
        _bias_kernel,
        out_shape=jax.ShapeDtypeStruct((heads, rows, width), _F32),
        name="rel_bias",
    )(rev)


def _attn_kernel(x_ref, gn_ref, wq_ref, k_ref, v_ref, bias_ref, wo_ref, o_ref,
                 q_scr, a_scr, s_scr, p_scr, *, heads, head_dim, group_chunks):
    tq = x_ref.shape[1]
    rows = group_chunks * CHUNK
    width = (group_chunks + N_LEFT_CHUNKS) * CHUNK
    pad = N_LEFT_CHUNKS * CHUNK
    pair = V7X_LANES // head_dim
    n_slots = s_scr.shape[0]
    i = pl.program_id(1)

    x = x_ref[0]
    h = _rms_norm(x, gn_ref[...]).astype(_BF16)
    q_scr[...] = (_dot(h, wq_ref[...]) * (head_dim ** -0.5 * LOG2E)).astype(_BF16)

    lane = lax.broadcasted_iota(jnp.int32, (1, V7X_LANES), 1)
    kcol = lax.broadcasted_iota(jnp.int32, (1, width), 1)

    def group_body(gi, carry, *, mask_pad):
        r0 = pl.multiple_of(gi * rows, rows)
        k0 = pl.multiple_of(i * tq + gi * rows, rows)
        valid = (k0 + kcol) >= pad
        denom = {}
        o_low = {}

        def lanes_of(hd):
            return slice((hd // pair) * V7X_LANES, (hd // pair + 1) * V7X_LANES)

        def mine_of(hd):
            sub = hd % pair
            return (lane >= sub * head_dim) & (lane < (sub + 1) * head_dim)

        def scores(hd):
            q2 = q_scr[pl.ds(r0, rows), lanes_of(hd)]
            qm = jnp.where(mine_of(hd), q2, jnp.zeros_like(q2))
            k2 = k_ref[0, pl.ds(k0, width), lanes_of(hd)]
            s = lax.dot_general(qm, k2, (((1,), (1,)), ((), ())),
                                preferred_element_type=_F32)
            s = s + bias_ref[hd]
            if mask_pad:
                s = jnp.where(valid, s, NEG_INF)
            s_scr[hd % n_slots] = s

        def softmax(hd):
            s = s_scr[hd % n_slots]
            p = jnp.exp2(s - jnp.max(s, axis=-1, keepdims=True))
            denom[hd] = jnp.sum(p, axis=-1, keepdims=True)
            p_scr[hd % n_slots] = p.astype(_BF16)

        def weighted_values(hd):
            v2 = v_ref[0, pl.ds(k0, width), lanes_of(hd)]
            o = _dot(p_scr[hd % n_slots], v2) * (1.0 / denom.pop(hd))
            if hd % pair == 0:
                o_low[hd // pair] = o
            else:
                o = jnp.where(mine_of(hd), o, o_low.pop(hd // pair))
                a_scr[pl.ds(r0, rows), lanes_of(hd)] = o.astype(_BF16)

        for t in range(heads + 2):
            if t < heads:
                scores(t)
            if 1 <= t <= heads:
                softmax(t - 1)
            if t >= 2:
                weighted_values(t - 2)
        return carry

    n_groups = tq // rows

    @pl.when(i * tq < pad)
    def _():
        lax.fori_loop(0, n_groups, functools.partial(group_body, mask_pad=True), 0)

    @pl.when(i * tq >= pad)
    def _():
        lax.fori_loop(0, n_groups, functools.partial(group_body, mask_pad=False), 0)

    o_ref[0] = x + _dot(a_scr[...], wo_ref[...])


def _attn_layer(x3, g_norm, w_q, bias, w_o, k_pad, v_pad, heads):
    b, s, d = x3.shape
    head_dim = d // heads
    tq = ATTN_TILE
    sp = k_pad.shape[1]
    rows = Q_GROUP_CHUNKS * CHUNK
    width = (Q_GROUP_CHUNKS + N_LEFT_CHUNKS) * CHUNK
    kern = functools.partial(_attn_kernel, heads=heads, head_dim=head_dim,
                             group_chunks=Q_GROUP_CHUNKS)
    kv_spec = pl.BlockSpec((1, sp, d), lambda bi, i: (bi, 0, 0), pipeline_mode=pl.Buffered(1))
    return pl.pallas_call(
        kern,
        grid=(b, s // tq),
        in_specs=[
            pl.BlockSpec((1, tq, d), lambda bi, i: (bi, i, 0)),
            _const_spec((1, d)),
            _const_spec((d, d)),
            kv_spec,
            kv_spec,
            _const_spec(bias.shape),
            _const_spec((d, d)),
        ],
        out_specs=pl.BlockSpec((1, tq, d), lambda bi, i: (bi, i, 0)),
        out_shape=jax.ShapeDtypeStruct((b, s, d), _F32),
        scratch_shapes=[pltpu.VMEM((tq, d), _BF16), pltpu.VMEM((tq, d), _BF16),
                        pltpu.VMEM((ATTN_SLOTS, rows, width), _F32),
                        pltpu.VMEM((ATTN_SLOTS, rows, width), _BF16)],
        compiler_params=pltpu.CompilerParams(
            dimension_semantics=("arbitrary", "arbitrary"),
            vmem_limit_bytes=V7X_VMEM_LIMIT_BYTES),
        name="banded_attention",
    )(x3, g_norm.reshape(1, d), w_q.astype(_BF16), k_pad, v_pad, bias, w_o.astype(_BF16))


def kernel(x, a_norm, a_w_in, a_sgu_norm, a_w_spatial, a_b_spatial, a_w_out, kv_norm, w_kv,
           b_norm, b_w_q, b_rel_bias, b_w_o, ffn_norm, ffn_w_gate_up, ffn_w_down, final_norm):
    b, s, d = x.shape
    n_a = a_norm.shape[0]
    depth = ffn_norm.shape[0]
    heads = b_rel_bias.shape[1]
    assert s % ATTN_TILE == 0 and (b * s) % TOKEN_TILE == 0
    assert ATTN_TILE % (Q_GROUP_CHUNKS * CHUNK) == 0 and TOKEN_TILE % A_CHUNK == 0

    x2 = x.reshape(b * s, d)
    k_pad = v_pad = None
    for layer in range(depth):
        if layer < n_a:
            x2 = _gmlp_layer(x2, a_norm[layer], a_w_in[layer], a_sgu_norm[layer],
                             a_w_spatial[layer], a_b_spatial[layer], a_w_out[layer])
        else:
            x3 = x2.reshape(b, s, d)
            if layer == n_a:
                k_pad, v_pad = _shared_kv(x3, kv_norm, w_kv)
            li = layer - n_a
            bias = _group_bias(b_rel_bias[li], Q_GROUP_CHUNKS)
            x2 = _attn_layer(x3, b_norm[li], b_w_q[li], bias, b_w_o[li], k_pad, v_pad,
                             heads).reshape(b * s, d)
        x2 = _ffn_layer(x2, ffn_norm[layer], ffn_w_gate_up[layer], ffn_w_down[layer],
                        final_norm, final_norm=(layer == depth - 1))
    return x2.reshape(b, s, d)
```

```python
import functools
import math

import jax
import jax.numpy as jnp
from jax import lax
from jax.experimental import pallas as pl
from jax.experimental.pallas import tpu as pltpu

EPS = 1e-6
CHUNK = 64
A_CHUNK = 128
N_LEFT_CHUNKS = 8
MAX_REL = 256
NEG_INF = -1e30
LOG2E = math.log2(math.e)

V7X_LANES = 128
V7X_MXU_DIM = 256
V7X_VMEM_LIMIT_BYTES = 56 * 1024 * 1024

TOKEN_TILE = 512
Q_GROUP_CHUNKS = 2
ATTN_TILE = 512
ATTN_SLOTS = 4

_BF16 = jnp.bfloat16
_F32 = jnp.float32


def _const_spec(shape):
    nd = len(shape)
    return pl.BlockSpec(shape, lambda *_: (0,) * nd, pipeline_mode=pl.Buffered(1))


def _rms_norm(x, g):
    return (x * lax.rsqrt(jnp.mean(x * x, axis=-1, keepdims=True) + EPS)) * g


def _dot(a, b):
    return jnp.dot(a, b, preferred_element_type=_F32)


def _gelu(x):
    return 0.5 * x * (1.0 + lax.erf(x * (1.0 / math.sqrt(2.0))))


def _gmlp_kernel(x_ref, gn_ref, win_ref, gsgu_ref, ws_ref, bs_ref, wout_ref,
                 o_ref, v_scr, t_scr, *, n_groups, gdim):
    tm = x_ref.shape[0]
    dff = n_groups * gdim
    x = x_ref[...]
    h = _rms_norm(x, gn_ref[...]).astype(_BF16)

    ss = jnp.zeros((tm, 1), _F32)
    for g in range(n_groups):
        z = _gelu(_dot(h, win_ref[:, dff + g * gdim: dff + (g + 1) * gdim]))
        v_scr[:, g * gdim:(g + 1) * gdim] = z
        ss = ss + jnp.sum(z * z, axis=-1, keepdims=True)
    r = lax.rsqrt(ss * (1.0 / dff) + EPS)

    row = lax.broadcasted_iota(jnp.int32, (A_CHUNK, A_CHUNK), 0) // CHUNK
    col = lax.broadcasted_iota(jnp.int32, (A_CHUNK, A_CHUNK), 1) // CHUNK
    causal = row >= col

    for g in range(n_groups):
        cols = slice(g * gdim, (g + 1) * gdim)
        vn = ((v_scr[:, cols] * r) * gsgu_ref[:, cols]).astype(_BF16)
        w = jnp.where(causal, ws_ref[g], 0.0).astype(_BF16)
        sp = jnp.concatenate(
            [_dot(w, vn[k * A_CHUNK:(k + 1) * A_CHUNK]) for k in range(tm // A_CHUNK)], axis=0)
        sp = sp + bs_ref[:, g:g + 1]
        u = _gelu(_dot(h, win_ref[:, cols]))
        t_scr[:, cols] = (u * sp).astype(_BF16)

    o_ref[...] = x + _dot(t_scr[...], wout_ref[...])


def _gmlp_layer(x2, g_norm, w_in, g_sgu, w_s, b_s, w_out):
    t, d = x2.shape
    n_groups = w_s.shape[0]
    dff = w_out.shape[0]
    gdim = dff // n_groups
    tm = TOKEN_TILE
    bs_tile = jnp.tile(b_s.T, (tm // A_CHUNK, 1))
    kern = functools.partial(_gmlp_kernel, n_groups=n_groups, gdim=gdim)
    return pl.pallas_call(
        kern,
        grid=(t // tm,),
        in_specs=[
            pl.BlockSpec((tm, d), lambda i: (i, 0)),
            _const_spec((1, d)),
            _const_spec((d, 2 * dff)),
            _const_spec((1, dff)),
            _const_spec(w_s.shape),
            _const_spec((tm, n_groups)),
            _const_spec((dff, d)),
        ],
        out_specs=pl.BlockSpec((tm, d), lambda i: (i, 0)),
        out_shape=jax.ShapeDtypeStruct((t, d), _F32),
        scratch_shapes=[pltpu.VMEM((tm, dff), _F32), pltpu.VMEM((tm, dff), _BF16)],
        compiler_params=pltpu.CompilerParams(
            dimension_semantics=("arbitrary",), vmem_limit_bytes=V7X_VMEM_LIMIT_BYTES),
        name="gmlp_mixer",
    )(x2, g_norm.reshape(1, d), w_in.astype(_BF16), g_sgu.reshape(1, dff), w_s, bs_tile,
      w_out.astype(_BF16))


def _ffn_kernel(x_ref, gn_ref, wgu_ref, wd_ref, fn_ref, o_ref, a_scr, *, hidden, final_norm):
    x = x_ref[...]
    h = _rms_norm(x, gn_ref[...]).astype(_BF16)
    ck = V7X_MXU_DIM
    for c in range(hidden // ck):
        gate = _dot(h, wgu_ref[:, c * ck:(c + 1) * ck])
        up = _dot(h, wgu_ref[:, hidden + c * ck: hidden + (c + 1) * ck])
        a_scr[:, c * ck:(c + 1) * ck] = ((gate * jax.nn.sigmoid(gate)) * up).astype(_BF16)
    y = x + _dot(a_scr[...], wd_ref[...])
    if final_norm:
        y = _rms_norm(y, fn_ref[...])
    o_ref[...] = y


def _ffn_layer(x2, g_norm, w_gate_up, w_down, g_final, final_norm):
    t, d = x2.shape
    hidden = w_down.shape[0]
    assert hidden % V7X_MXU_DIM == 0
    tm = TOKEN_TILE
    kern = functools.partial(_ffn_kernel, hidden=hidden, final_norm=final_norm)
    return pl.pallas_call(
        kern,
        grid=(t // tm,),
        in_specs=[
            pl.BlockSpec((tm, d), lambda i: (i, 0)),
            _const_spec((1, d)),
            _const_spec((d, 2 * hidden)),
            _const_spec((hidden, d)),
            _const_spec((1, d)),
        ],
        out_specs=pl.BlockSpec((tm, d), lambda i: (i, 0)),
        out_shape=jax.ShapeDtypeStruct((t, d), _F32),
        scratch_shapes=[pltpu.VMEM((tm, hidden), _BF16)],
        compiler_params=pltpu.CompilerParams(
            dimension_semantics=("arbitrary",), vmem_limit_bytes=V7X_VMEM_LIMIT_BYTES),
        name="swiglu_ffn",
    )(x2, g_norm.reshape(1, d), w_gate_up.astype(_BF16), w_down.astype(_BF16),
      g_final.reshape(1, d))


def _kv_kernel(x_ref, gn_ref, w_ref, k_ref, v_ref):
    j = pl.program_id(1)
    d = x_ref.shape[-1]

    @pl.when(j == 0)
    def _():
        k_ref[...] = jnp.zeros_like(k_ref)
        v_ref[...] = jnp.zeros_like(v_ref)

    @pl.when(j > 0)
    def _():
        h = _rms_norm(x_ref[0], gn_ref[...]).astype(_BF16)
        kv = _dot(h, w_ref[...])
        k_ref[0] = kv[:, :d].astype(_BF16)
        v_ref[0] = kv[:, d:].astype(_BF16)


def _shared_kv(x3, g_norm, w_kv):
    b, s, d = x3.shape
    pad = N_LEFT_CHUNKS * CHUNK
    tk = pad
    assert s % tk == 0
    out = jax.ShapeDtypeStruct((b, s + pad, d), _BF16)
    return pl.pallas_call(
        _kv_kernel,
        grid=(b, 1 + s // tk),
        in_specs=[
            pl.BlockSpec((1, tk, d), lambda bi, j: (bi, jnp.maximum(j - 1, 0), 0)),
            _const_spec((1, d)),
            _const_spec((d, 2 * d)),
        ],
        out_specs=[pl.BlockSpec((1, tk, d), lambda bi, j: (bi, j, 0))] * 2,
        out_shape=[out, out],
        compiler_params=pltpu.CompilerParams(
            dimension_semantics=("arbitrary", "arbitrary"),
            vmem_limit_bytes=V7X_VMEM_LIMIT_BYTES),
        name="shared_kv",
    )(x3, g_norm.reshape(1, d), w_kv.astype(_BF16))


def _bias_kernel(r_ref, o_ref):
    _, rows, width = o_ref.shape
    qc = lax.broadcasted_iota(jnp.int32, (1, width), 1) // CHUNK
    for r in range(rows):
        j = r // CHUNK
        in_band = (qc >= j) & (qc <= j + N_LEFT_CHUNKS)
        off = rows - 1 - r
        o_ref[:, r, :] = jnp.where(in_band, r_ref[:, off:off + width] * LOG2E, NEG_INF)


def _group_bias(rel_table, group_chunks):
    heads = rel_table.shape[0]
    rows = group_chunks * CHUNK
    width = (group_chunks + N_LEFT_CHUNKS) * CHUNK
    dist = (rows - 1 + N_LEFT_CHUNKS * CHUNK) - jnp.arange(rows - 1 + width)
    rev = rel_table[:, jnp.clip(dist, -MAX_REL, MAX_REL) + MAX_REL]
    return pl.pallas_call(
        _bias_kernel,
        out_shape=jax.ShapeDtypeStruct((heads, rows, width), _F32),
        name="rel_bias",
    )(rev)


def _attn_kernel(x_ref, gn_ref, wq_ref, k_ref, v_ref, bias_ref, wo_ref, o_ref,
                 q_scr, a_scr, s_scr, p_scr, *, heads, head_dim, group_chunks):
    tq = x_ref.shape[1]
    rows = group_chunks * CHUNK
    width = (group_chunks + N_LEFT_CHUNKS) * CHUNK
    pad = N_LEFT_CHUNKS * CHUNK
    pair = V7X_LANES // head_dim
    n_slots = s_scr.shape[0]
    i = pl.program_id(1)

    x = x_ref[0]
    h = _rms_norm(x, gn_ref[...]).astype(_BF16)
    q_scr[...] = (_dot(h, wq_ref[...]) * (head_dim ** -0.5 * LOG2E)).astype(_BF16)

    lane = lax.broadcasted_iota(jnp.int32, (1, V7X_LANES), 1)
    kcol = lax.broadcasted_iota(jnp.int32, (1, width), 1)

    def group_body(gi, carry, *, mask_pad):
        r0 = pl.multiple_of(gi * rows, rows)
        k0 = pl.multiple_of(i * tq + gi * rows, rows)
        valid = (k0 + kcol) >= pad
        denom = {}

        def lanes_of(slab):
            return slice(slab * V7X_LANES, (slab + 1) * V7X_LANES)

        def mine_of(sub):
            return (lane >= sub * head_dim) & (lane < (sub + 1) * head_dim)

        def scores(slab):
            q2 = q_scr[pl.ds(r0, rows), lanes_of(slab)]
            qm = jnp.concatenate(
                [jnp.where(mine_of(sub), q2, jnp.zeros_like(q2)) for sub in range(pair)], axis=0)
            k2 = k_ref[0, pl.ds(k0, width), lanes_of(slab)]
            s = lax.dot_general(qm, k2, (((1,), (1,)), ((), ())),
                                preferred_element_type=_F32)
            s = s + bias_ref[slab]
            if mask_pad:
                s = jnp.where(valid, s, NEG_INF)
            s_scr[slab % n_slots] = s

        def softmax(slab):
            s = s_scr[slab % n_slots]
            p = jnp.exp2(s - jnp.max(s, axis=-1, keepdims=True))
            denom[slab] = jnp.sum(p, axis=-1, keepdims=True)
            p_scr[slab % n_slots] = p.astype(_BF16)

        def weighted_values(slab):
            v2 = v_ref[0, pl.ds(k0, width), lanes_of(slab)]
            o = _dot(p_scr[slab % n_slots], v2) * (1.0 / denom.pop(slab))
            out = o[:rows]
            for sub in range(1, pair):
                out = jnp.where(mine_of(sub), o[sub * rows:(sub + 1) * rows], out)
            a_scr[pl.ds(r0, rows), lanes_of(slab)] = out.astype(_BF16)

        n_slabs = heads // pair
        for t in range(n_slabs + 2):
            if t < n_slabs:
                scores(t)
            if 1 <= t <= n_slabs:
                softmax(t - 1)
            if t >= 2:
                weighted_values(t - 2)
        return carry

    n_groups = tq // rows

    @pl.when(i * tq < pad)
    def _():
        lax.fori_loop(0, n_groups, functools.partial(group_body, mask_pad=True), 0)

    @pl.when(i * tq >= pad)
    def _():
        lax.fori_loop(0, n_groups, functools.partial(group_body, mask_pad=False), 0)

    o_ref[0] = x + _dot(a_scr[...], wo_ref[...])


def _attn_layer(x3, g_norm, w_q, bias, w_o, k_pad, v_pad, heads):
    b, s, d = x3.shape
    head_dim = d // heads
    tq = ATTN_TILE
    sp = k_pad.shape[1]
    rows = Q_GROUP_CHUNKS * CHUNK
    width = (Q_GROUP_CHUNKS + N_LEFT_CHUNKS) * CHUNK
    pair = V7X_LANES // head_dim
    bias = bias.reshape(heads // pair, pair * rows, width)
    kern = functools.partial(_attn_kernel, heads=heads, head_dim=head_dim,
                             group_chunks=Q_GROUP_CHUNKS)
    kv_spec = pl.BlockSpec((1, sp, d), lambda bi, i: (bi, 0, 0), pipeline_mode=pl.Buffered(1))
    return pl.pallas_call(
        kern,
        grid=(b, s // tq),
        in_specs=[
            pl.BlockSpec((1, tq, d), lambda bi, i: (bi, i, 0)),
            _const_spec((1, d)),
            _const_spec((d, d)),
            kv_spec,
            kv_spec,
            _const_spec(bias.shape),
            _const_spec((d, d)),
        ],
        out_specs=pl.BlockSpec((1, tq, d), lambda bi, i: (bi, i, 0)),
        out_shape=jax.ShapeDtypeStruct((b, s, d), _F32),
        scratch_shapes=[pltpu.VMEM((tq, d), _BF16), pltpu.VMEM((tq, d), _BF16),
                        pltpu.VMEM((ATTN_SLOTS, pair * rows, width), _F32),
                        pltpu.VMEM((ATTN_SLOTS, pair * rows, width), _BF16)],
        compiler_params=pltpu.CompilerParams(
            dimension_semantics=("arbitrary", "arbitrary"),
            vmem_limit_bytes=V7X_VMEM_LIMIT_BYTES),
        name="banded_attention",
    )(x3, g_norm.reshape(1, d), w_q.astype(_BF16), k_pad, v_pad, bias, w_o.astype(_BF16))


def kernel(x, a_norm, a_w_in, a_sgu_norm, a_w_spatial, a_b_spatial, a_w_out, kv_norm, w_kv,
           b_norm, b_w_q, b_rel_bias, b_w_o, ffn_norm, ffn_w_gate_up, ffn_w_down, final_norm):
    b, s, d = x.shape
    n_a = a_norm.shape[0]
    depth = ffn_norm.shape[0]
    heads = b_rel_bias.shape[1]
    assert s % ATTN_TILE == 0 and (b * s) % TOKEN_TILE == 0
    assert ATTN_TILE % (Q_GROUP_CHUNKS * CHUNK) == 0 and TOKEN_TILE % A_CHUNK == 0

    x2 = x.reshape(b * s, d)
    k_pad = v_pad = None
    for layer in range(depth):
        if layer < n_a:
            x2 = _gmlp_layer(x2, a_norm[layer], a_w_in[layer], a_sgu_norm[layer],
                             a_w_spatial[layer], a_b_spatial[layer], a_w_out[layer])
        else:
            x3 = x2.reshape(b, s, d)
            if layer == n_a:
                k_pad, v_pad = _shared_kv(x3, kv_norm, w_kv)
            li = layer - n_a
            bias = _group_bias(b_rel_bias[li], Q_GROUP_CHUNKS)
            x2 = _attn_layer(x3, b_norm[li], b_w_q[li], bias, b_w_o[li], k_pad, v_pad,
                             heads).reshape(b * s, d)
        x2 = _ffn_layer(x2, ffn_norm[layer], ffn_w_gate_up[layer], ffn_w_down[layer],
                        final_norm, final_norm=(layer == depth - 1))
    return x2.reshape(b, s, d)
```

```python
import functools
import math

import jax
import jax.numpy as jnp
from jax import lax
from jax.experimental import pallas as pl
from jax.experimental.pallas import tpu as pltpu

EPS = 1e-6
CHUNK = 64
A_CHUNK = 128
N_LEFT_CHUNKS = 8
MAX_REL = 256
NEG_INF = -1e30
LOG2E = math.log2(math.e)

V7X_LANES = 128
V7X_MXU_DIM = 256
V7X_VMEM_LIMIT_BYTES = 56 * 1024 * 1024

TOKEN_TILE = 512
Q_GROUP_CHUNKS = 2
ATTN_TILE = 512
ATTN_SLOTS = 4

_BF16 = jnp.bfloat16
_F32 = jnp.float32


def _const_spec(shape):
    nd = len(shape)
    return pl.BlockSpec(shape, lambda *_: (0,) * nd, pipeline_mode=pl.Buffered(1))


def _layer_spec(stacked_shape, layer):
    nd = len(stacked_shape)
    return pl.BlockSpec((None,) + tuple(stacked_shape[1:]),
                        lambda *_: (layer,) + (0,) * (nd - 1), pipeline_mode=pl.Buffered(1))


def _rms_norm(x, g):
    return (x * lax.rsqrt(jnp.mean(x * x, axis=-1, keepdims=True) + EPS)) * g


def _dot(a, b):
    return jnp.dot(a, b, preferred_element_type=_F32)


def _gelu(x):
    return 0.5 * x * (1.0 + lax.erf(x * (1.0 / math.sqrt(2.0))))


def _gmlp_kernel(x_ref, gn_ref, win_ref, gsgu_ref, ws_ref, bs_ref, wout_ref,
                 o_ref, v_scr, t_scr, *, n_groups, gdim):
    tm = x_ref.shape[0]
    dff = n_groups * gdim
    x = x_ref[...]
    h = _rms_norm(x, gn_ref[...]).astype(_BF16)

    ss = jnp.zeros((tm, 1), _F32)
    for g in range(n_groups):
        z = _gelu(_dot(h, win_ref[:, dff + g * gdim: dff + (g + 1) * gdim]))
        v_scr[:, g * gdim:(g + 1) * gdim] = z
        ss = ss + jnp.sum(z * z, axis=-1, keepdims=True)
    r = lax.rsqrt(ss * (1.0 / dff) + EPS)

    row = lax.broadcasted_iota(jnp.int32, (A_CHUNK, A_CHUNK), 0) // CHUNK
    col = lax.broadcasted_iota(jnp.int32, (A_CHUNK, A_CHUNK), 1) // CHUNK
    causal = row >= col

    for g in range(n_groups):
        cols = slice(g * gdim, (g + 1) * gdim)
        vn = ((v_scr[:, cols] * r) * gsgu_ref[:, cols]).astype(_BF16)
        w = jnp.where(causal, ws_ref[g], 0.0).astype(_BF16)
        bias = bs_ref[:, g:g + 1]
        sp = jnp.concatenate(
            [_dot(w, vn[k * A_CHUNK:(k + 1) * A_CHUNK]) + bias for k in range(tm // A_CHUNK)],
            axis=0)
        u = _gelu(_dot(h, win_ref[:, cols]))
        t_scr[:, cols] = (u * sp).astype(_BF16)

    o_ref[...] = x + _dot(t_scr[...], wout_ref[...])


def _gmlp_layer(x2, layer, g_norm, w_in, g_sgu, w_s, b_s_t, w_out):
    t, d = x2.shape
    n_groups = w_s.shape[1]
    dff = w_out.shape[1]
    gdim = dff // n_groups
    tm = TOKEN_TILE
    kern = functools.partial(_gmlp_kernel, n_groups=n_groups, gdim=gdim)
    return pl.pallas_call(
        kern,
        grid=(t // tm,),
        in_specs=[
            pl.BlockSpec((tm, d), lambda i: (i, 0)),
            _layer_spec(g_norm.shape, layer),
            _layer_spec(w_in.shape, layer),
            _layer_spec(g_sgu.shape, layer),
            _layer_spec(w_s.shape, layer),
            _layer_spec(b_s_t.shape, layer),
            _layer_spec(w_out.shape, layer),
        ],
        out_specs=pl.BlockSpec((tm, d), lambda i: (i, 0)),
        out_shape=jax.ShapeDtypeStruct((t, d), _F32),
        scratch_shapes=[pltpu.VMEM((tm, dff), _F32), pltpu.VMEM((tm, dff), _BF16)],
        compiler_params=pltpu.CompilerParams(
            dimension_semantics=("arbitrary",), vmem_limit_bytes=V7X_VMEM_LIMIT_BYTES),
        name="gmlp_mixer",
    )(x2, g_norm, w_in, g_sgu, w_s, b_s_t, w_out)


def _ffn_kernel(x_ref, gn_ref, wgu_ref, wd_ref, fn_ref, o_ref, a_scr, *, hidden, final_norm):
    x = x_ref[...]
    h = _rms_norm(x, gn_ref[...]).astype(_BF16)
    ck = V7X_MXU_DIM
    for c in range(hidden // ck):
        gate = _dot(h, wgu_ref[:, c * ck:(c + 1) * ck])
        up = _dot(h, wgu_ref[:, hidden + c * ck: hidden + (c + 1) * ck])
        a_scr[:, c * ck:(c + 1) * ck] = ((gate * jax.nn.sigmoid(gate)) * up).astype(_BF16)
    y = x + _dot(a_scr[...], wd_ref[...])
    if final_norm:
        y = _rms_norm(y, fn_ref[...])
    o_ref[...] = y


def _ffn_layer(x2, layer, g_norm, w_gate_up, w_down, g_final, final_norm):
    t, d = x2.shape
    hidden = w_down.shape[1]
    assert hidden % V7X_MXU_DIM == 0
    tm = TOKEN_TILE
    kern = functools.partial(_ffn_kernel, hidden=hidden, final_norm=final_norm)
    return pl.pallas_call(
        kern,
        grid=(t // tm,),
        in_specs=[
            pl.BlockSpec((tm, d), lambda i: (i, 0)),
            _layer_spec(g_norm.shape, layer),
            _layer_spec(w_gate_up.shape, layer),
            _layer_spec(w_down.shape, layer),
            _const_spec((1, d)),
        ],
        out_specs=pl.BlockSpec((tm, d), lambda i: (i, 0)),
        out_shape=jax.ShapeDtypeStruct((t, d), _F32),
        scratch_shapes=[pltpu.VMEM((tm, hidden), _BF16)],
        compiler_params=pltpu.CompilerParams(
            dimension_semantics=("arbitrary",), vmem_limit_bytes=V7X_VMEM_LIMIT_BYTES),
        name="swiglu_ffn",
    )(x2, g_norm, w_gate_up, w_down, g_final)


def _kv_kernel(x_ref, gn_ref, w_ref, k_ref, v_ref):
    j = pl.program_id(1)
    d = x_ref.shape[-1]

    @pl.when(j == 0)
    def _():
        k_ref[...] = jnp.zeros_like(k_ref)
        v_ref[...] = jnp.zeros_like(v_ref)

    @pl.when(j > 0)
    def _():
        h = _rms_norm(x_ref[0], gn_ref[...]).astype(_BF16)
        kv = _dot(h, w_ref[...])
        k_ref[0] = kv[:, :d].astype(_BF16)
        v_ref[0] = kv[:, d:].astype(_BF16)


def _shared_kv(x3, g_norm, w_kv):
    b, s, d = x3.shape
    pad = N_LEFT_CHUNKS * CHUNK
    tk = pad
    assert s % tk == 0
    out = jax.ShapeDtypeStruct((b, s + pad, d), _BF16)
    return pl.pallas_call(
        _kv_kernel,
        grid=(b, 1 + s // tk),
        in_specs=[
            pl.BlockSpec((1, tk, d), lambda bi, j: (bi, jnp.maximum(j - 1, 0), 0)),
            _const_spec((1, d)),
            _const_spec((d, 2 * d)),
        ],
        out_specs=[pl.BlockSpec((1, tk, d), lambda bi, j: (bi, j, 0))] * 2,
        out_shape=[out, out],
        compiler_params=pltpu.CompilerParams(
            dimension_semantics=("arbitrary", "arbitrary"),
            vmem_limit_bytes=V7X_VMEM_LIMIT_BYTES),
        name="shared_kv",
    )(x3, g_norm.reshape(1, d), w_kv.astype(_BF16))


def _bias_kernel(r_ref, o_ref):
    _, rows, width = o_ref.shape
    qc = lax.broadcasted_iota(jnp.int32, (1, width), 1) // CHUNK
    for r in range(rows):
        j = r // CHUNK
        in_band = (qc >= j) & (qc <= j + N_LEFT_CHUNKS)
        off = rows - 1 - r
        o_ref[:, r, :] = jnp.where(in_band, r_ref[:, off:off + width] * LOG2E, NEG_INF)


def _group_bias(rel_table, group_chunks):
    heads = rel_table.shape[0]
    rows = group_chunks * CHUNK
    width = (group_chunks + N_LEFT_CHUNKS) * CHUNK
    dist = (rows - 1 + N_LEFT_CHUNKS * CHUNK) - jnp.arange(rows - 1 + width)
    rev = rel_table[:, jnp.clip(dist, -MAX_REL, MAX_REL) + MAX_REL]
    return pl.pallas_call(
        _bias_kernel,
        out_shape=jax.ShapeDtypeStruct((heads, rows, width), _F32),
        name="rel_bias",
    )(rev)


def _attn_kernel(x_ref, gn_ref, wq_ref, klo_ref, khi_ref, vlo_ref, vhi_ref, bias_ref, wo_ref,
                 o_ref, q_scr, a_scr, k_scr, v_scr, s_scr, p_scr,
                 *, heads, head_dim, group_chunks):
    tq = x_ref.shape[1]
    rows = group_chunks * CHUNK
    width = (group_chunks + N_LEFT_CHUNKS) * CHUNK
    pad = N_LEFT_CHUNKS * CHUNK
    pair = V7X_LANES // head_dim
    n_slots = s_scr.shape[0]
    i = pl.program_id(1)

    k_scr[:tq] = klo_ref[0]
    k_scr[tq:] = khi_ref[0]
    v_scr[:tq] = vlo_ref[0]
    v_scr[tq:] = vhi_ref[0]

    x = x_ref[0]
    h = _rms_norm(x, gn_ref[...]).astype(_BF16)
    q_scr[...] = (_dot(h, wq_ref[...]) * (head_dim ** -0.5 * LOG2E)).astype(_BF16)

    lane = lax.broadcasted_iota(jnp.int32, (1, V7X_LANES), 1)
    kcol = lax.broadcasted_iota(jnp.int32, (1, width), 1)

    def group_body(gi, carry, *, mask_pad):
        r0 = pl.multiple_of(gi * rows, rows)
        valid = (i * tq + r0 + kcol) >= pad
        denom = {}

        def lanes_of(slab):
            return slice(slab * V7X_LANES, (slab + 1) * V7X_LANES)

        def mine_of(sub):
            return (lane >= sub * head_dim) & (lane < (sub + 1) * head_dim)

        def scores(slab):
            q2 = q_scr[pl.ds(r0, rows), lanes_of(slab)]
            qm = jnp.concatenate(
                [jnp.where(mine_of(sub), q2, jnp.zeros_like(q2)) for sub in range(pair)], axis=0)
            k2 = k_scr[pl.ds(r0, width), lanes_of(slab)]
            s = lax.dot_general(qm, k2, (((1,), (1,)), ((), ())),
                                preferred_element_type=_F32)
            s = s + bias_ref[slab]
            if mask_pad:
                s = jnp.where(valid, s, NEG_INF)
            s_scr[slab % n_slots] = s

        def softmax(slab):
            s = s_scr[slab % n_slots]
            p = jnp.exp2(s - jnp.max(s, axis=-1, keepdims=True))
            denom[slab] = jnp.sum(p, axis=-1, keepdims=True)
            p_scr[slab % n_slots] = p.astype(_BF16)

        def weighted_values(slab):
            v2 = v_scr[pl.ds(r0, width), lanes_of(slab)]
            o = _dot(p_scr[slab % n_slots], v2) * (1.0 / denom.pop(slab))
            out = o[:rows]
            for sub in range(1, pair):
                out = jnp.where(mine_of(sub), o[sub * rows:(sub + 1) * rows], out)
            a_scr[pl.ds(r0, rows), lanes_of(slab)] = out.astype(_BF16)

        n_slabs = heads // pair
        for t in range(n_slabs + 2):
            if t < n_slabs:
                scores(t)
            if 1 <= t <= n_slabs:
                softmax(t - 1)
            if t >= 2:
                weighted_values(t - 2)
        return carry

    n_groups = tq // rows

    @pl.when(i * tq < pad)
    def _():
        lax.fori_loop(0, n_groups, functools.partial(group_body, mask_pad=True), 0)

    @pl.when(i * tq >= pad)
    def _():
        lax.fori_loop(0, n_groups, functools.partial(group_body, mask_pad=False), 0)

    o_ref[0] = x + _dot(a_scr[...], wo_ref[...])


def _attn_layer(x3, layer, g_norm, w_q, bias, w_o, k_pad, v_pad, heads):
    b, s, d = x3.shape
    head_dim = d // heads
    tq = ATTN_TILE
    rows = Q_GROUP_CHUNKS * CHUNK
    width = (Q_GROUP_CHUNKS + N_LEFT_CHUNKS) * CHUNK
    pair = V7X_LANES // head_dim
    assert N_LEFT_CHUNKS * CHUNK <= tq
    bias = bias.reshape(heads // pair, pair * rows, width)
    kern = functools.partial(_attn_kernel, heads=heads, head_dim=head_dim,
                             group_chunks=Q_GROUP_CHUNKS)
    lo_spec = pl.BlockSpec((1, tq, d), lambda bi, i: (bi, i, 0))
    hi_spec = pl.BlockSpec((1, tq, d), lambda bi, i: (bi, i + 1, 0))
    return pl.pallas_call(
        kern,
        grid=(b, s // tq),
        in_specs=[
            pl.BlockSpec((1, tq, d), lambda bi, i: (bi, i, 0)),
            _layer_spec(g_norm.shape, layer),
            _layer_spec(w_q.shape, layer),
            lo_spec, hi_spec, lo_spec, hi_spec,
            _const_spec(bias.shape),
            _layer_spec(w_o.shape, layer),
        ],
        out_specs=pl.BlockSpec((1, tq, d), lambda bi, i: (bi, i, 0)),
        out_shape=jax.ShapeDtypeStruct((b, s, d), _F32),
        scratch_shapes=[pltpu.VMEM((tq, d), _BF16), pltpu.VMEM((tq, d), _BF16),
                        pltpu.VMEM((2 * tq, d), _BF16), pltpu.VMEM((2 * tq, d), _BF16),
                        pltpu.VMEM((ATTN_SLOTS, pair * rows, width), _F32),
                        pltpu.VMEM((ATTN_SLOTS, pair * rows, width), _BF16)],
        compiler_params=pltpu.CompilerParams(
            dimension_semantics=("arbitrary", "arbitrary"),
            vmem_limit_bytes=V7X_VMEM_LIMIT_BYTES),
        name="banded_attention",
    )(x3, g_norm, w_q, k_pad, k_pad, v_pad, v_pad, bias, w_o)


def kernel(x, a_norm, a_w_in, a_sgu_norm, a_w_spatial, a_b_spatial, a_w_out, kv_norm, w_kv,
           b_norm, b_w_q, b_rel_bias, b_w_o, ffn_norm, ffn_w_gate_up, ffn_w_down, final_norm):
    b, s, d = x.shape
    n_a = a_norm.shape[0]
    depth = ffn_norm.shape[0]
    heads = b_rel_bias.shape[1]
    assert s % ATTN_TILE == 0 and (b * s) % TOKEN_TILE == 0
    assert ATTN_TILE % (Q_GROUP_CHUNKS * CHUNK) == 0 and TOKEN_TILE % A_CHUNK == 0

    a_norm3 = a_norm[:, None, :]
    a_sgu3 = a_sgu_norm[:, None, :]
    a_bs_t = jnp.swapaxes(a_b_spatial, 1, 2)
    a_w_in16, a_w_out16 = a_w_in.astype(_BF16), a_w_out.astype(_BF16)
    b_norm3 = b_norm[:, None, :]
    b_w_q16, b_w_o16 = b_w_q.astype(_BF16), b_w_o.astype(_BF16)
    ffn_norm3 = ffn_norm[:, None, :]
    ffn_wgu16, ffn_wd16 = ffn_w_gate_up.astype(_BF16), ffn_w_down.astype(_BF16)
    g_final = final_norm.reshape(1, d)

    x2 = x.reshape(b * s, d)
    k_pad = v_pad = None
    for layer in range(depth):
        if layer < n_a:
            x2 = _gmlp_layer(x2, layer, a_norm3, a_w_in16, a_sgu3, a_w_spatial, a_bs_t, a_w_out16)
        else:
            x3 = x2.reshape(b, s, d)
            if layer == n_a:
                k_pad, v_pad = _shared_kv(x3, kv_norm, w_kv)
            li = layer - n_a
            bias = _group_bias(b_rel_bias[li], Q_GROUP_CHUNKS)
            x2 = _attn_layer(x3, li, b_norm3, b_w_q16, bias, b_w_o16, k_pad, v_pad,
                             heads).reshape(b * s, d)
        x2 = _ffn_layer(x2, layer, ffn_norm3, ffn_wgu16, ffn_wd16, g_final,
                        final_norm=(layer == depth - 1))
    return x2.reshape(b, s, d)
```

```python
import functools
import math

import jax
import jax.numpy as jnp
from jax import lax
from jax.experimental import pallas as pl
from jax.experimental.pallas import tpu as pltpu

EPS = 1e-6
CHUNK = 64
A_CHUNK = 128
N_LEFT_CHUNKS = 8
MAX_REL = 256
NEG_INF = -1e30
LOG2E = math.log2(math.e)

V7X_LANES = 128
V7X_MXU_DIM = 256
V7X_VMEM_LIMIT_BYTES = 56 * 1024 * 1024

TOKEN_TILE = 512
Q_GROUP_CHUNKS = 2
ATTN_TILE = 512
ATTN_SLOTS = 4
STAGE_DEPTH = 2

_BF16 = jnp.bfloat16
_F32 = jnp.float32


def _const_spec(shape):
    nd = len(shape)
    return pl.BlockSpec(shape, lambda *_: (0,) * nd, pipeline_mode=pl.Buffered(1))


def _layer_spec(stacked_shape, layer):
    nd = len(stacked_shape)
    return pl.BlockSpec((None,) + tuple(stacked_shape[1:]),
                        lambda *_: (layer,) + (0,) * (nd - 1), pipeline_mode=pl.Buffered(1))


def _rms_norm(x, g):
    return (x * lax.rsqrt(jnp.mean(x * x, axis=-1, keepdims=True) + EPS)) * g


def _dot(a, b):
    return jnp.dot(a, b, preferred_element_type=_F32)


def _gelu(x):
    return 0.5 * x * (1.0 + lax.erf(x * (1.0 / math.sqrt(2.0))))


class _WeightStream:
    def __init__(self, sources, stage, sems):
        self.sources, self.stage, self.sems = sources, stage, sems
        self.depth = stage.shape[0]

    def _copy(self, k):
        slot = k % self.depth
        return pltpu.make_async_copy(self.sources[k], self.stage.at[slot], self.sems.at[slot])

    def prime(self):
        for k in range(min(self.depth, len(self.sources))):
            self._copy(k).start()

    def take(self, k, dst):
        self._copy(k).wait()
        dst[...] = self.stage[k % self.depth].astype(_BF16)
        if k + self.depth < len(self.sources):
            self._copy(k + self.depth).start()


def _gmlp_kernel(x_ref, gn_ref, win_hbm, gsgu_ref, ws_ref, bs_ref, wout_hbm, o_ref,
                 win16, wout16, v_scr, t_scr, in_stage, out_stage, sems,
                 *, layer, n_groups, gdim):
    tm = x_ref.shape[0]
    dff = n_groups * gdim

    def gate_cols(g):
        return slice(dff + g * gdim, dff + (g + 1) * gdim)

    def lin_cols(g):
        return slice(g * gdim, (g + 1) * gdim)

    def body(load_weights):
        if load_weights:
            in_chunks = [gate_cols(g) for g in range(n_groups)] + [lin_cols(g) for g in range(n_groups)]
            w_in = _WeightStream([win_hbm.at[layer, :, c] for c in in_chunks], in_stage, sems.at[0])
            w_out = _WeightStream([wout_hbm.at[layer, lin_cols(g), :] for g in range(n_groups)],
                                  out_stage, sems.at[1])
            w_in.prime()
            w_out.prime()

        x = x_ref[...]
        h = _rms_norm(x, gn_ref[...]).astype(_BF16)

        ss = jnp.zeros((tm, 1), _F32)
        for g in range(n_groups):
            if load_weights:
                w_in.take(g, win16.at[:, gate_cols(g)])
            z = _gelu(_dot(h, win16[:, gate_cols(g)]))
            v_scr[:, lin_cols(g)] = z
            ss = ss + jnp.sum(z * z, axis=-1, keepdims=True)
        r = lax.rsqrt(ss * (1.0 / dff) + EPS)

        row = lax.broadcasted_iota(jnp.int32, (A_CHUNK, A_CHUNK), 0) // CHUNK
        col = lax.broadcasted_iota(jnp.int32, (A_CHUNK, A_CHUNK), 1) // CHUNK
        causal = row >= col

        for g in range(n_groups):
            cols = lin_cols(g)
            if load_weights:
                w_in.take(n_groups + g, win16.at[:, cols])
                w_out.take(g, wout16.at[cols, :])
            vn = ((v_scr[:, cols] * r) * gsgu_ref[:, cols]).astype(_BF16)
            w = jnp.where(causal, ws_ref[g], 0.0).astype(_BF16)
            bias = bs_ref[:, g:g + 1]
            sp = jnp.concatenate(
                [_dot(w, vn[k * A_CHUNK:(k + 1) * A_CHUNK]) + bias for k in range(tm // A_CHUNK)],
                axis=0)
            u = _gelu(_dot(h, win16[:, cols]))
            t_scr[:, cols] = (u * sp).astype(_BF16)

        o_ref[...] = x + _dot(t_scr[...], wout16[...])

    first = pl.program_id(0) == 0
    pl.when(first)(functools.partial(body, True))
    pl.when(jnp.logical_not(first))(functools.partial(body, False))


def _gmlp_layer(x2, layer, g_norm, w_in, g_sgu, w_s, b_s_t, w_out):
    t, d = x2.shape
    n_groups = w_s.shape[1]
    dff = w_out.shape[1]
    gdim = dff // n_groups
    tm = TOKEN_TILE
    kern = functools.partial(_gmlp_kernel, layer=layer, n_groups=n_groups, gdim=gdim)
    return pl.pallas_call(
        kern,
        grid=(t // tm,),
        in_specs=[
            pl.BlockSpec((tm, d), lambda i: (i, 0)),
            _layer_spec(g_norm.shape, layer),
            pl.BlockSpec(memory_space=pl.ANY),
            _layer_spec(g_sgu.shape, layer),
            _layer_spec(w_s.shape, layer),
            _layer_spec(b_s_t.shape, layer),
            pl.BlockSpec(memory_space=pl.ANY),
        ],
        out_specs=pl.BlockSpec((tm, d), lambda i: (i, 0)),
        out_shape=jax.ShapeDtypeStruct((t, d), _F32),
        scratch_shapes=[pltpu.VMEM((d, 2 * dff), _BF16), pltpu.VMEM((dff, d), _BF16),
                        pltpu.VMEM((tm, dff), _F32), pltpu.VMEM((tm, dff), _BF16),
                        pltpu.VMEM((STAGE_DEPTH, d, gdim), _F32),
                        pltpu.VMEM((STAGE_DEPTH, gdim, d), _F32),
                        pltpu.SemaphoreType.DMA((2, STAGE_DEPTH))],
        compiler_params=pltpu.CompilerParams(
            dimension_semantics=("arbitrary",), vmem_limit_bytes=V7X_VMEM_LIMIT_BYTES),
        name="gmlp_mixer",
    )(x2, g_norm, w_in, g_sgu, w_s, b_s_t, w_out)


def _ffn_kernel(x_ref, gn_ref, wgu_hbm, wd_hbm, fn_ref, o_ref,
                wgu16, wd16, a_scr, gu_stage, d_stage, sems, *, layer, hidden, final_norm):
    ck = V7X_MXU_DIM
    n_chunks = hidden // ck

    def gate_cols(c):
        return slice(c * ck, (c + 1) * ck)

    def up_cols(c):
        return slice(hidden + c * ck, hidden + (c + 1) * ck)

    def body(load_weights):
        if load_weights:
            gu_chunks = [cols(c) for c in range(n_chunks) for cols in (gate_cols, up_cols)]
            w_gu = _WeightStream([wgu_hbm.at[layer, :, c] for c in gu_chunks], gu_stage, sems.at[0])
            w_d = _WeightStream([wd_hbm.at[layer, gate_cols(c), :] for c in range(n_chunks)],
                                d_stage, sems.at[1])
            w_gu.prime()
            w_d.prime()

        x = x_ref[...]
        h = _rms_norm(x, gn_ref[...]).astype(_BF16)
        for c in range(n_chunks):
            if load_weights:
                w_gu.take(2 * c, wgu16.at[:, gate_cols(c)])
                w_gu.take(2 * c + 1, wgu16.at[:, up_cols(c)])
                w_d.take(c, wd16.at[gate_cols(c), :])
            gate = _dot(h, wgu16[:, gate_cols(c)])
            up = _dot(h, wgu16[:, up_cols(c)])
            a_scr[:, gate_cols(c)] = ((gate * jax.nn.sigmoid(gate)) * up).astype(_BF16)
        y = x + _dot(a_scr[...], wd16[...])
        if final_norm:
            y = _rms_norm(y, fn_ref[...])
        o_ref[...] = y

    first = pl.program_id(0) == 0
    pl.when(first)(functools.partial(body, True))
    pl.when(jnp.logical_not(first))(functools.partial(body, False))


def _ffn_layer(x2, layer, g_norm, w_gate_up, w_down, g_final, final_norm):
    t, d = x2.shape
    hidden = w_down.shape[1]
    ck = V7X_MXU_DIM
    assert hidden % ck == 0
    tm = TOKEN_TILE
    kern = functools.partial(_ffn_kernel, layer=layer, hidden=hidden, final_norm=final_norm)
    return pl.pallas_call(
        kern,
        grid=(t // tm,),
        in_specs=[
            pl.BlockSpec((tm, d), lambda i: (i, 0)),
            _layer_spec(g_norm.shape, layer),
            pl.BlockSpec(memory_space=pl.ANY),
            pl.BlockSpec(memory_space=pl.ANY),
            _const_spec((1, d)),
        ],
        out_specs=pl.BlockSpec((tm, d), lambda i: (i, 0)),
        out_shape=jax.ShapeDtypeStruct((t, d), _F32),
        scratch_shapes=[pltpu.VMEM((d, 2 * hidden), _BF16), pltpu.VMEM((hidden, d), _BF16),
                        pltpu.VMEM((tm, hidden), _BF16),
                        pltpu.VMEM((2 * STAGE_DEPTH, d, ck), _F32),
                        pltpu.VMEM((STAGE_DEPTH, ck, d), _F32),
                        pltpu.SemaphoreType.DMA((2, 2 * STAGE_DEPTH))],
        compiler_params=pltpu.CompilerParams(
            dimension_semantics=("arbitrary",), vmem_limit_bytes=V7X_VMEM_LIMIT_BYTES),
        name="swiglu_ffn",
    )(x2, g_norm, w_gate_up, w_down, g_final)


def _kv_kernel(x_ref, gn_ref, w_ref, k_ref, v_ref):
    j = pl.program_id(1)
    d = x_ref.shape[-1]

    @pl.when(j == 0)
    def _():
        k_ref[...] = jnp.zeros_like(k_ref)
        v_ref[...] = jnp.zeros_like(v_ref)

    @pl.when(j > 0)
    def _():
        h = _rms_norm(x_ref[0], gn_ref[...]).astype(_BF16)
        kv = _dot(h, w_ref[...])
        k_ref[0] = kv[:, :d].astype(_BF16)
        v_ref[0] = kv[:, d:].astype(_BF16)


def _shared_kv(x3, g_norm, w_kv):
    b, s, d = x3.shape
    pad = N_LEFT_CHUNKS * CHUNK
    tk = pad
    assert s % tk == 0
    out = jax.ShapeDtypeStruct((b, s + pad, d), _BF16)
    return pl.pallas_call(
        _kv_kernel,
        grid=(b, 1 + s // tk),
        in_specs=[
            pl.BlockSpec((1, tk, d), lambda bi, j: (bi, jnp.maximum(j - 1, 0), 0)),
            _const_spec((1, d)),
            _const_spec((d, 2 * d)),
        ],
        out_specs=[pl.BlockSpec((1, tk, d), lambda bi, j: (bi, j, 0))] * 2,
        out_shape=[out, out],
        compiler_params=pltpu.CompilerParams(
            dimension_semantics=("arbitrary", "arbitrary"),
            vmem_limit_bytes=V7X_VMEM_LIMIT_BYTES),
        name="shared_kv",
    )(x3, g_norm.reshape(1, d), w_kv.astype(_BF16))


def _bias_kernel(r_ref, o_ref):
    _, rows, width = o_ref.shape
    qc = lax.broadcasted_iota(jnp.int32, (1, width), 1) // CHUNK
    for r in range(rows):
        j = r // CHUNK
        in_band = (qc >= j) & (qc <= j + N_LEFT_CHUNKS)
        off = rows - 1 - r
        o_ref[:, r, :] = jnp.where(in_band, r_ref[:, off:off + width] * LOG2E, NEG_INF)


def _group_bias(rel_table, group_chunks):
    heads = rel_table.shape[0]
    rows = group_chunks * CHUNK
    width = (group_chunks + N_LEFT_CHUNKS) * CHUNK
    dist = (rows - 1 + N_LEFT_CHUNKS * CHUNK) - jnp.arange(rows - 1 + width)
    rev = rel_table[:, jnp.clip(dist, -MAX_REL, MAX_REL) + MAX_REL]
    return pl.pallas_call(
        _bias_kernel,
        out_shape=jax.ShapeDtypeStruct((heads, rows, width), _F32),
        name="rel_bias",
    )(rev)


def _attn_kernel(x_ref, gn_ref, wq_ref, klo_ref, khi_ref, vlo_ref, vhi_ref, bias_ref, wo_ref,
                 o_ref, q_scr, a_scr, k_scr, v_scr, s_scr, p_scr,
                 *, heads, head_dim, group_chunks):
    tq = x_ref.shape[1]
    rows = group_chunks * CHUNK
    width = (group_chunks + N_LEFT_CHUNKS) * CHUNK
    pad = N_LEFT_CHUNKS * CHUNK
    pair = V7X_LANES // head_dim
    n_slots = s_scr.shape[0]
    i = pl.program_id(1)

    k_scr[:tq] = klo_ref[0]
    k_scr[tq:] = khi_ref[0]
    v_scr[:tq] = vlo_ref[0]
    v_scr[tq:] = vhi_ref[0]

    x = x_ref[0]
    h = _rms_norm(x, gn_ref[...]).astype(_BF16)
    q_scr[...] = (_dot(h, wq_ref[...]) * (head_dim ** -0.5 * LOG2E)).astype(_BF16)

    lane = lax.broadcasted_iota(jnp.int32, (1, V7X_LANES), 1)
    kcol = lax.broadcasted_iota(jnp.int32, (1, width), 1)

    def group_body(gi, carry, *, mask_pad):
        r0 = pl.multiple_of(gi * rows, rows)
        valid = (i * tq + r0 + kcol) >= pad
        denom = {}

        def lanes_of(slab):
            return slice(slab * V7X_LANES, (slab + 1) * V7X_LANES)

        def mine_of(sub):
            return (lane >= sub * head_dim) & (lane < (sub + 1) * head_dim)

        def scores(slab):
            q2 = q_scr[pl.ds(r0, rows), lanes_of(slab)]
            qm = jnp.concatenate(
                [jnp.where(mine_of(sub), q2, jnp.zeros_like(q2)) for sub in range(pair)], axis=0)
            k2 = k_scr[pl.ds(r0, width), lanes_of(slab)]
            s = lax.dot_general(qm, k2, (((1,), (1,)), ((), ())),
                                preferred_element_type=_F32)
            s = s + bias_ref[slab]
            if mask_pad:
                s = jnp.where(valid, s, NEG_INF)
            s_scr[slab % n_slots] = s

        def softmax(slab):
            s = s_scr[slab % n_slots]
            p = jnp.exp2(s - jnp.max(s, axis=-1, keepdims=True))
            denom[slab] = jnp.sum(p, axis=-1, keepdims=True)
            p_scr[slab % n_slots] = p.astype(_BF16)

        def weighted_values(slab):
            v2 = v_scr[pl.ds(r0, width), lanes_of(slab)]
            o = _dot(p_scr[slab % n_slots], v2) * (1.0 / denom.pop(slab))
            out = o[:rows]
            for sub in range(1, pair):
                out = jnp.where(mine_of(sub), o[sub * rows:(sub + 1) * rows], out)
            a_scr[pl.ds(r0, rows), lanes_of(slab)] = out.astype(_BF16)

        n_slabs = heads // pair
        for t in range(n_slabs + 2):
            if t < n_slabs:
                scores(t)
            if 1 <= t <= n_slabs:
                softmax(t - 1)
            if t >= 2:
                weighted_values(t - 2)
        return carry

    n_groups = tq // rows

    @pl.when(i * tq < pad)
    def _():
        lax.fori_loop(0, n_groups, functools.partial(group_body, mask_pad=True), 0)

    @pl.when(i * tq >= pad)
    def _():
        lax.fori_loop(0, n_groups, functools.partial(group_body, mask_pad=False), 0)

    o_ref[0] = x + _dot(a_scr[...], wo_ref[...])


def _attn_layer(x3, layer, g_norm, w_q, bias, w_o, k_pad, v_pad, heads):
    b, s, d = x3.shape
    head_dim = d // heads
    tq = ATTN_TILE
    rows = Q_GROUP_CHUNKS * CHUNK
    width = (Q_GROUP_CHUNKS + N_LEFT_CHUNKS) * CHUNK
    pair = V7X_LANES // head_dim
    assert N_LEFT_CHUNKS * CHUNK <= tq
    bias = bias.reshape(heads // pair, pair * rows, width)
    kern = functools.partial(_attn_kernel, heads=heads, head_dim=head_dim,
                             group_chunks=Q_GROUP_CHUNKS)
    lo_spec = pl.BlockSpec((1, tq, d), lambda bi, i: (bi, i, 0))
    hi_spec = pl.BlockSpec((1, tq, d), lambda bi, i: (bi, i + 1, 0))
    return pl.pallas_call(
        kern,
        grid=(b, s // tq),
        in_specs=[
            pl.BlockSpec((1, tq, d), lambda bi, i: (bi, i, 0)),
            _layer_spec(g_norm.shape, layer),
            _layer_spec(w_q.shape, layer),
            lo_spec, hi_spec, lo_spec, hi_spec,
            _const_spec(bias.shape),
            _layer_spec(w_o.shape, layer),
        ],
        out_specs=pl.BlockSpec((1, tq, d), lambda bi, i: (bi, i, 0)),
        out_shape=jax.ShapeDtypeStruct((b, s, d), _F32),
        scratch_shapes=[pltpu.VMEM((tq, d), _BF16), pltpu.VMEM((tq, d), _BF16),
                        pltpu.VMEM((2 * tq, d), _BF16), pltpu.VMEM((2 * tq, d), _BF16),
                        pltpu.VMEM((ATTN_SLOTS, pair * rows, width), _F32),
                        pltpu.VMEM((ATTN_SLOTS, pair * rows, width), _BF16)],
        compiler_params=pltpu.CompilerParams(
            dimension_semantics=("arbitrary", "arbitrary"),
            vmem_limit_bytes=V7X_VMEM_LIMIT_BYTES),
        name="banded_attention",
    )(x3, g_norm, w_q, k_pad, k_pad, v_pad, v_pad, bias, w_o)


def kernel(x, a_norm, a_w_in, a_sgu_norm, a_w_spatial, a_b_spatial, a_w_out, kv_norm, w_kv,
           b_norm, b_w_q, b_rel_bias, b_w_o, ffn_norm, ffn_w_gate_up, ffn_w_down, final_norm):
    b, s, d = x.shape
    n_a = a_norm.shape[0]
    depth = ffn_norm.shape[0]
    heads = b_rel_bias.shape[1]
    assert s % ATTN_TILE == 0 and (b * s) % TOKEN_TILE == 0
    assert ATTN_TILE % (Q_GROUP_CHUNKS * CHUNK) == 0 and TOKEN_TILE % A_CHUNK == 0

    a_norm3 = a_norm[:, None, :]
    a_sgu3 = a_sgu_norm[:, None, :]
    a_bs_t = jnp.swapaxes(a_b_spatial, 1, 2)
    b_norm3 = b_norm[:, None, :]
    b_w_q16, b_w_o16 = b_w_q.astype(_BF16), b_w_o.astype(_BF16)
    ffn_norm3 = ffn_norm[:, None, :]
    g_final = final_norm.reshape(1, d)

    x2 = x.reshape(b * s, d)
    k_pad = v_pad = None
    for layer in range(depth):
        if layer < n_a:
            x2 = _gmlp_layer(x2, layer, a_norm3, a_w_in, a_sgu3, a_w_spatial, a_bs_t, a_w_out)
        else:
            x3 = x2.reshape(b, s, d)
            if layer == n_a:
                k_pad, v_pad = _shared_kv(x3, kv_norm, w_kv)
            li = layer - n_a
            bias = _group_bias(b_rel_bias[li], Q_GROUP_CHUNKS)
            x2 = _attn_layer(x3, li, b_norm3, b_w_q16, bias, b_w_o16, k_pad, v_pad,
                             heads).reshape(b * s, d)
        x2 = _ffn_layer(x2, layer, ffn_norm3, ffn_w_gate_up, ffn_w_down, g_final,
                        final_norm=(layer == depth - 1))
    return x2.reshape(b, s, d)
```

```python
import functools
import math

import jax
import jax.numpy as jnp
from jax import lax
from jax.experimental import pallas as pl
from jax.experimental.pallas import tpu as pltpu

EPS = 1e-6
CHUNK = 64
A_CHUNK = 128
N_LEFT_CHUNKS = 8
MAX_REL = 256
NEG_INF = -1e30
LOG2E = math.log2(math.e)

V7X_LANES = 128
V7X_MXU_DIM = 256
V7X_VMEM_LIMIT_BYTES = 56 * 1024 * 1024

TOKEN_TILE = 512
Q_GROUP_CHUNKS = 2
ATTN_TILE = 512
ATTN_SLOTS = 4
STAGE_DEPTH = 3

_BF16 = jnp.bfloat16
_F32 = jnp.float32


def _const_spec(shape):
    nd = len(shape)
    return pl.BlockSpec(shape, lambda *_: (0,) * nd, pipeline_mode=pl.Buffered(1))


def _layer_spec(stacked_shape, layer):
    nd = len(stacked_shape)
    return pl.BlockSpec((None,) + tuple(stacked_shape[1:]),
                        lambda *_: (layer,) + (0,) * (nd - 1), pipeline_mode=pl.Buffered(1))


def _rms_norm(x, g):
    return (x * lax.rsqrt(jnp.mean(x * x, axis=-1, keepdims=True) + EPS)) * g


def _dot(a, b):
    return jnp.dot(a, b, preferred_element_type=_F32)


def _gelu(x):
    return 0.5 * x * (1.0 + lax.erf(x * (1.0 / math.sqrt(2.0))))


class _WeightStream:
    def __init__(self, sources, stage, sems):
        self.sources, self.stage, self.sems = sources, stage, sems
        self.depth = stage.shape[0]

    def _copy(self, k):
        slot = k % self.depth
        return pltpu.make_async_copy(self.sources[k], self.stage.at[slot], self.sems.at[slot])

    def prime(self):
        for k in range(min(self.depth, len(self.sources))):
            self._copy(k).start()

    def take(self, k, dst):
        self._copy(k).wait()
        dst[...] = self.stage[k % self.depth].astype(_BF16)
        if k + self.depth < len(self.sources):
            self._copy(k + self.depth).start()


def _gmlp_kernel(x_ref, gn_ref, win_hbm, gsgu_ref, ws_ref, bs_ref, wout_hbm, o_ref,
                 win16, wout16, v_scr, t_scr, in_stage, out_stage, in_sems, out_sems,
                 *, layer, n_groups, gdim):
    tm = x_ref.shape[0]
    dff = n_groups * gdim

    def gate_cols(g):
        return slice(dff + g * gdim, dff + (g + 1) * gdim)

    def lin_cols(g):
        return slice(g * gdim, (g + 1) * gdim)

    def body(load_weights):
        if load_weights:
            in_chunks = [gate_cols(g) for g in range(n_groups)] + [lin_cols(g) for g in range(n_groups)]
            w_in = _WeightStream([win_hbm.at[layer, :, c] for c in in_chunks], in_stage, in_sems)
            w_out = _WeightStream([wout_hbm.at[layer, lin_cols(g), :] for g in range(n_groups)],
                                  out_stage, out_sems)
            w_in.prime()
            w_out.prime()

        x = x_ref[...]
        h = _rms_norm(x, gn_ref[...]).astype(_BF16)

        ss = jnp.zeros((tm, 1), _F32)
        for g in range(n_groups):
            if load_weights:
                w_in.take(g, win16.at[:, gate_cols(g)])
            z = _gelu(_dot(h, win16[:, gate_cols(g)]))
            v_scr[:, lin_cols(g)] = z
            ss = ss + jnp.sum(z * z, axis=-1, keepdims=True)
        r = lax.rsqrt(ss * (1.0 / dff) + EPS)

        row = lax.broadcasted_iota(jnp.int32, (A_CHUNK, A_CHUNK), 0) // CHUNK
        col = lax.broadcasted_iota(jnp.int32, (A_CHUNK, A_CHUNK), 1) // CHUNK
        causal = row >= col

        for g in range(n_groups):
            cols = lin_cols(g)
            if load_weights:
                w_in.take(n_groups + g, win16.at[:, cols])
                w_out.take(g, wout16.at[cols, :])
            vn = ((v_scr[:, cols] * r) * gsgu_ref[:, cols]).astype(_BF16)
            w = jnp.where(causal, ws_ref[g], 0.0).astype(_BF16)
            bias = bs_ref[:, g:g + 1]
            sp = jnp.concatenate(
                [_dot(w, vn[k * A_CHUNK:(k + 1) * A_CHUNK]) + bias for k in range(tm // A_CHUNK)],
                axis=0)
            u = _gelu(_dot(h, win16[:, cols]))
            t_scr[:, cols] = (u * sp).astype(_BF16)

        o_ref[...] = x + _dot(t_scr[...], wout16[...])

    first = pl.program_id(0) == 0
    pl.when(first)(functools.partial(body, True))
    pl.when(jnp.logical_not(first))(functools.partial(body, False))


def _gmlp_layer(x2, layer, g_norm, w_in, g_sgu, w_s, b_s_t, w_out):
    t, d = x2.shape
    n_groups = w_s.shape[1]
    dff = w_out.shape[1]
    gdim = dff // n_groups
    tm = TOKEN_TILE
    kern = functools.partial(_gmlp_kernel, layer=layer, n_groups=n_groups, gdim=gdim)
    return pl.pallas_call(
        kern,
        grid=(t // tm,),
        in_specs=[
            pl.BlockSpec((tm, d), lambda i: (i, 0)),
            _layer_spec(g_norm.shape, layer),
            pl.BlockSpec(memory_space=pl.ANY),
            _layer_spec(g_sgu.shape, layer),
            _layer_spec(w_s.shape, layer),
            _layer_spec(b_s_t.shape, layer),
            pl.BlockSpec(memory_space=pl.ANY),
        ],
        out_specs=pl.BlockSpec((tm, d), lambda i: (i, 0)),
        out_shape=jax.ShapeDtypeStruct((t, d), _F32),
        scratch_shapes=[pltpu.VMEM((d, 2 * dff), _BF16), pltpu.VMEM((dff, d), _BF16),
                        pltpu.VMEM((tm, dff), _F32), pltpu.VMEM((tm, dff), _BF16),
                        pltpu.VMEM((2 * STAGE_DEPTH, d, gdim), _F32),
                        pltpu.VMEM((STAGE_DEPTH, gdim, d), _F32),
                        pltpu.SemaphoreType.DMA((2 * STAGE_DEPTH,)),
                        pltpu.SemaphoreType.DMA((STAGE_DEPTH,))],
        compiler_params=pltpu.CompilerParams(
            dimension_semantics=("arbitrary",), vmem_limit_bytes=V7X_VMEM_LIMIT_BYTES),
        name="gmlp_mixer",
    )(x2, g_norm, w_in, g_sgu, w_s, b_s_t, w_out)


def _ffn_kernel(x_ref, gn_ref, wgu_hbm, wd_hbm, fn_ref, o_ref,
                wgu16, wd16, a_scr, gu_stage, d_stage, gu_sems, d_sems,
                *, layer, hidden, final_norm):
    ck = V7X_MXU_DIM
    n_chunks = hidden // ck

    def gate_cols(c):
        return slice(c * ck, (c + 1) * ck)

    def up_cols(c):
        return slice(hidden + c * ck, hidden + (c + 1) * ck)

    def body(load_weights):
        if load_weights:
            gu_chunks = [cols(c) for c in range(n_chunks) for cols in (gate_cols, up_cols)]
            w_gu = _WeightStream([wgu_hbm.at[layer, :, c] for c in gu_chunks], gu_stage, gu_sems)
            w_d = _WeightStream([wd_hbm.at[layer, gate_cols(c), :] for c in range(n_chunks)],
                                d_stage, d_sems)
            w_gu.prime()
            w_d.prime()

        x = x_ref[...]
        h = _rms_norm(x, gn_ref[...]).astype(_BF16)
        for c in range(n_chunks):
            if load_weights:
                w_gu.take(2 * c, wgu16.at[:, gate_cols(c)])
                w_gu.take(2 * c + 1, wgu16.at[:, up_cols(c)])
                w_d.take(c, wd16.at[gate_cols(c), :])
            gate = _dot(h, wgu16[:, gate_cols(c)])
            up = _dot(h, wgu16[:, up_cols(c)])
            a_scr[:, gate_cols(c)] = ((gate * jax.nn.sigmoid(gate)) * up).astype(_BF16)
        y = x + _dot(a_scr[...], wd16[...])
        if final_norm:
            y = _rms_norm(y, fn_ref[...])
        o_ref[...] = y

    first = pl.program_id(0) == 0
    pl.when(first)(functools.partial(body, True))
    pl.when(jnp.logical_not(first))(functools.partial(body, False))


def _ffn_layer(x2, layer, g_norm, w_gate_up, w_down, g_final, final_norm):
    t, d = x2.shape
    hidden = w_down.shape[1]
    ck = V7X_MXU_DIM
    assert hidden % ck == 0
    tm = TOKEN_TILE
    kern = functools.partial(_ffn_kernel, layer=layer, hidden=hidden, final_norm=final_norm)
    return pl.pallas_call(
        kern,
        grid=(t // tm,),
        in_specs=[
            pl.BlockSpec((tm, d), lambda i: (i, 0)),
            _layer_spec(g_norm.shape, layer),
            pl.BlockSpec(memory_space=pl.ANY),
            pl.BlockSpec(memory_space=pl.ANY),
            _const_spec((1, d)),
        ],
        out_specs=pl.BlockSpec((tm, d), lambda i: (i, 0)),
        out_shape=jax.ShapeDtypeStruct((t, d), _F32),
        scratch_shapes=[pltpu.VMEM((d, 2 * hidden), _BF16), pltpu.VMEM((hidden, d), _BF16),
                        pltpu.VMEM((tm, hidden), _BF16),
                        pltpu.VMEM((2 * STAGE_DEPTH, d, ck), _F32),
                        pltpu.VMEM((STAGE_DEPTH, ck, d), _F32),
                        pltpu.SemaphoreType.DMA((2 * STAGE_DEPTH,)),
                        pltpu.SemaphoreType.DMA((STAGE_DEPTH,))],
        compiler_params=pltpu.CompilerParams(
            dimension_semantics=("arbitrary",), vmem_limit_bytes=V7X_VMEM_LIMIT_BYTES),
        name="swiglu_ffn",
    )(x2, g_norm, w_gate_up, w_down, g_final)


def _kv_kernel(x_ref, gn_ref, w_ref, k_ref, v_ref):
    j = pl.program_id(1)
    d = x_ref.shape[-1]

    @pl.when(j == 0)
    def _():
        k_ref[...] = jnp.zeros_like(k_ref)
        v_ref[...] = jnp.zeros_like(v_ref)

    @pl.when(j > 0)
    def _():
        h = _rms_norm(x_ref[0], gn_ref[...]).astype(_BF16)
        kv = _dot(h, w_ref[...])
        k_ref[0] = kv[:, :d].astype(_BF16)
        v_ref[0] = kv[:, d:].astype(_BF16)


def _shared_kv(x3, g_norm, w_kv):
    b, s, d = x3.shape
    pad = N_LEFT_CHUNKS * CHUNK
    tk = pad
    assert s % tk == 0
    out = jax.ShapeDtypeStruct((b, s + pad, d), _BF16)
    return pl.pallas_call(
        _kv_kernel,
        grid=(b, 1 + s // tk),
        in_specs=[
            pl.BlockSpec((1, tk, d), lambda bi, j: (bi, jnp.maximum(j - 1, 0), 0)),
            _const_spec((1, d)),
            _const_spec((d, 2 * d)),
        ],
        out_specs=[pl.BlockSpec((1, tk, d), lambda bi, j: (bi, j, 0))] * 2,
        out_shape=[out, out],
        compiler_params=pltpu.CompilerParams(
            dimension_semantics=("arbitrary", "arbitrary"),
            vmem_limit_bytes=V7X_VMEM_LIMIT_BYTES),
        name="shared_kv",
    )(x3, g_norm.reshape(1, d), w_kv.astype(_BF16))


def _bias_kernel(r_ref, o_ref):
    _, rows, width = o_ref.shape
    qc = lax.broadcasted_iota(jnp.int32, (1, width), 1) // CHUNK
    for r in range(rows):
        j = r // CHUNK
        in_band = (qc >= j) & (qc <= j + N_LEFT_CHUNKS)
        off = rows - 1 - r
        o_ref[:, r, :] = jnp.where(in_band, r_ref[:, off:off + width] * LOG2E, NEG_INF)


def _group_bias(rel_table, group_chunks):
    heads = rel_table.shape[0]
    rows = group_chunks * CHUNK
    width = (group_chunks + N_LEFT_CHUNKS) * CHUNK
    dist = (rows - 1 + N_LEFT_CHUNKS * CHUNK) - jnp.arange(rows - 1 + width)
    rev = rel_table[:, jnp.clip(dist, -MAX_REL, MAX_REL) + MAX_REL]
    return pl.pallas_call(
        _bias_kernel,
        out_shape=jax.ShapeDtypeStruct((heads, rows, width), _F32),
        name="rel_bias",
    )(rev)


def _attn_kernel(x_ref, gn_ref, wq_ref, klo_ref, khi_ref, vlo_ref, vhi_ref, bias_ref, wo_ref,
                 o_ref, q_scr, a_scr, k_scr, v_scr, s_scr, p_scr,
                 *, heads, head_dim, group_chunks):
    tq = x_ref.shape[1]
    rows = group_chunks * CHUNK
    width = (group_chunks + N_LEFT_CHUNKS) * CHUNK
    pad = N_LEFT_CHUNKS * CHUNK
    pair = V7X_LANES // head_dim
    n_slots = s_scr.shape[0]
    i = pl.program_id(1)

    k_scr[:tq] = klo_ref[0]
    k_scr[tq:] = khi_ref[0]
    v_scr[:tq] = vlo_ref[0]
    v_scr[tq:] = vhi_ref[0]

    x = x_ref[0]
    h = _rms_norm(x, gn_ref[...]).astype(_BF16)
    q_scr[...] = (_dot(h, wq_ref[...]) * (head_dim ** -0.5 * LOG2E)).astype(_BF16)

    lane = lax.broadcasted_iota(jnp.int32, (1, V7X_LANES), 1)
    kcol = lax.broadcasted_iota(jnp.int32, (1, width), 1)

    def group_body(gi, carry, *, mask_pad):
        r0 = pl.multiple_of(gi * rows, rows)
        valid = (i * tq + r0 + kcol) >= pad
        denom = {}

        def lanes_of(slab):
            return slice(slab * V7X_LANES, (slab + 1) * V7X_LANES)

        def mine_of(sub):
            return (lane >= sub * head_dim) & (lane < (sub + 1) * head_dim)

        def scores(slab):
            q2 = q_scr[pl.ds(r0, rows), lanes_of(slab)]
            qm = jnp.concatenate(
                [jnp.where(mine_of(sub), q2, jnp.zeros_like(q2)) for sub in range(pair)], axis=0)
            k2 = k_scr[pl.ds(r0, width), lanes_of(slab)]
            s = lax.dot_general(qm, k2, (((1,), (1,)), ((), ())),
                                preferred_element_type=_F32)
            s = s + bias_ref[slab]
            if mask_pad:
                s = jnp.where(valid, s, NEG_INF)
            s_scr[slab % n_slots] = s

        def softmax(slab):
            s = s_scr[slab % n_slots]
            p = jnp.exp2(s - jnp.max(s, axis=-1, keepdims=True))
            denom[slab] = jnp.sum(p, axis=-1, keepdims=True)
            p_scr[slab % n_slots] = p.astype(_BF16)

        def weighted_values(slab):
            v2 = v_scr[pl.ds(r0, width), lanes_of(slab)]
            o = _dot(p_scr[slab % n_slots], v2) * (1.0 / denom.pop(slab))
            out = o[:rows]
            for sub in range(1, pair):
                out = jnp.where(mine_of(sub), o[sub * rows:(sub + 1) * rows], out)
            a_scr[pl.ds(r0, rows), lanes_of(slab)] = out.astype(_BF16)

        n_slabs = heads // pair
        for t in range(n_slabs + 2):
            if t < n_slabs:
                scores(t)
            if 1 <= t <= n_slabs:
                softmax(t - 1)
            if t >= 2:
                weighted_values(t - 2)
        return carry

    n_groups = tq // rows

    @pl.when(i * tq < pad)
    def _():
        lax.fori_loop(0, n_groups, functools.partial(group_body, mask_pad=True), 0)

    @pl.when(i * tq >= pad)
    def _():
        lax.fori_loop(0, n_groups, functools.partial(group_body, mask_pad=False), 0)

    o_ref[0] = x + _dot(a_scr[...], wo_ref[...])


def _attn_layer(x3, layer, g_norm, w_q, bias, w_o, k_pad, v_pad, heads):
    b, s, d = x3.shape
    head_dim = d // heads
    tq = ATTN_TILE
    rows = Q_GROUP_CHUNKS * CHUNK
    width = (Q_GROUP_CHUNKS + N_LEFT_CHUNKS) * CHUNK
    pair = V7X_LANES // head_dim
    assert N_LEFT_CHUNKS * CHUNK <= tq
    bias = bias.reshape(heads // pair, pair * rows, width)
    kern = functools.partial(_attn_kernel, heads=heads, head_dim=head_dim,
                             group_chunks=Q_GROUP_CHUNKS)
    lo_spec = pl.BlockSpec((1, tq, d), lambda bi, i: (bi, i, 0))
    hi_spec = pl.BlockSpec((1, tq, d), lambda bi, i: (bi, i + 1, 0))
    return pl.pallas_call(
        kern,
        grid=(b, s // tq),
        in_specs=[
            pl.BlockSpec((1, tq, d), lambda bi, i: (bi, i, 0)),
            _layer_spec(g_norm.shape, layer),
            _layer_spec(w_q.shape, layer),
            lo_spec, hi_spec, lo_spec, hi_spec,
            _const_spec(bias.shape),
            _layer_spec(w_o.shape, layer),
        ],
        out_specs=pl.BlockSpec((1, tq, d), lambda bi, i: (bi, i, 0)),
        out_shape=jax.ShapeDtypeStruct((b, s, d), _F32),
        scratch_shapes=[pltpu.VMEM((tq, d), _BF16), pltpu.VMEM((tq, d), _BF16),
                        pltpu.VMEM((2 * tq, d), _BF16), pltpu.VMEM((2 * tq, d), _BF16),
                        pltpu.VMEM((ATTN_SLOTS, pair * rows, width), _F32),
                        pltpu.VMEM((ATTN_SLOTS, pair * rows, width), _BF16)],
        compiler_params=pltpu.CompilerParams(
            dimension_semantics=("arbitrary", "arbitrary"),
            vmem_limit_bytes=V7X_VMEM_LIMIT_BYTES),
        name="banded_attention",
    )(x3, g_norm, w_q, k_pad, k_pad, v_pad, v_pad, bias, w_o)


def kernel(x, a_norm, a_w_in, a_sgu_norm, a_w_spatial, a_b_spatial, a_w_out, kv_norm, w_kv,
           b_norm, b_w_q, b_rel_bias, b_w_o, ffn_norm, ffn_w_gate_up, ffn_w_down, final_norm):
    b, s, d = x.shape
    n_a = a_norm.shape[0]
    depth = ffn_norm.shape[0]
    heads = b_rel_bias.shape[1]
    assert s % ATTN_TILE == 0 and (b * s) % TOKEN_TILE == 0
    assert ATTN_TILE % (Q_GROUP_CHUNKS * CHUNK) == 0 and TOKEN_TILE % A_CHUNK == 0

    a_norm3 = a_norm[:, None, :]
    a_sgu3 = a_sgu_norm[:, None, :]
    a_bs_t = jnp.swapaxes(a_b_spatial, 1, 2)
    b_norm3 = b_norm[:, None, :]
    b_w_q16, b_w_o16 = b_w_q.astype(_BF16), b_w_o.astype(_BF16)
    ffn_norm3 = ffn_norm[:, None, :]
    g_final = final_norm.reshape(1, d)

    x2 = x.reshape(b * s, d)
    k_pad = v_pad = None
    for layer in range(depth):
        if layer < n_a:
            x2 = _gmlp_layer(x2, layer, a_norm3, a_w_in, a_sgu3, a_w_spatial, a_bs_t, a_w_out)
        else:
            x3 = x2.reshape(b, s, d)
            if layer == n_a:
                k_pad, v_pad = _shared_kv(x3, kv_norm, w_kv)
            li = layer - n_a
            bias = _group_bias(b_rel_bias[li], Q_GROUP_CHUNKS)
            x2 = _attn_layer(x3, li, b_norm3, b_w_q16, bias, b_w_o16, k_pad, v_pad,
                             heads).reshape(b * s, d)
        x2 = _ffn_layer(x2, layer, ffn_norm3, ffn_w_gate_up, ffn_w_down, g_final,
                        final_norm=(layer == depth - 1))
    return x2.reshape(b, s, d)
```

```python
import functools
import math

import jax
import jax.numpy as jnp
from jax import lax
from jax.experimental import pallas as pl
from jax.experimental.pallas import tpu as pltpu

EPS = 1e-6
CHUNK = 64
A_CHUNK = 128
N_LEFT_CHUNKS = 8
MAX_REL = 256
NEG_INF = -1e30
LOG2E = math.log2(math.e)

V7X_LANES = 128
V7X_MXU_DIM = 256
V7X_VMEM_LIMIT_BYTES = 56 * 1024 * 1024

TOKEN_TILE = 512
Q_GROUP_CHUNKS = 2
ATTN_TILE = 512
ATTN_SLOTS = 4
STAGE_DEPTH = 3
GMLP_STAGE_DEPTH = 4
GMLP_IN_STAGE_ROWS = 128
GMLP_OUT_STAGE_ROWS = 256

_BF16 = jnp.bfloat16
_F32 = jnp.float32


def _const_spec(shape):
    nd = len(shape)
    return pl.BlockSpec(shape, lambda *_: (0,) * nd, pipeline_mode=pl.Buffered(1))


def _layer_spec(stacked_shape, layer):
    nd = len(stacked_shape)
    return pl.BlockSpec((None,) + tuple(stacked_shape[1:]),
                        lambda *_: (layer,) + (0,) * (nd - 1), pipeline_mode=pl.Buffered(1))


def _rms_norm(x, g):
    return (x * lax.rsqrt(jnp.mean(x * x, axis=-1, keepdims=True) + EPS)) * g


def _dot(a, b):
    return jnp.dot(a, b, preferred_element_type=_F32)


def _gelu(x):
    return 0.5 * x * (1.0 + lax.erf(x * (1.0 / math.sqrt(2.0))))


class _WeightStream:
    def __init__(self, sources, stage, sems):
        self.sources, self.stage, self.sems = sources, stage, sems
        self.depth = stage.shape[0]

    def _copy(self, k):
        slot = k % self.depth
        return pltpu.make_async_copy(self.sources[k], self.stage.at[slot], self.sems.at[slot])

    def prime(self):
        for k in range(min(self.depth, len(self.sources))):
            self._copy(k).start()

    def take(self, k, dst):
        self._copy(k).wait()
        dst[...] = self.stage[k % self.depth].astype(_BF16)
        if k + self.depth < len(self.sources):
            self._copy(k + self.depth).start()


def _gmlp_kernel(x_ref, gn_ref, win_hbm, gsgu_ref, ws_ref, bs_ref, wout_hbm, o_ref,
                 win16, wout16, v_scr, t_scr, in_stage, out_stage, in_sems, out_sems,
                 *, layer, n_groups, gdim):
    tm, d = x_ref.shape
    dff = n_groups * gdim
    first = pl.program_id(0) == 0

    in_rows = in_stage.shape[1]
    out_rows = out_stage.shape[1]
    w_in = _WeightStream([win_hbm.at[layer, pl.ds(k * in_rows, in_rows), :] for k in range(d // in_rows)],
                         in_stage, in_sems)
    w_out = _WeightStream(
        [wout_hbm.at[layer, pl.ds(k * out_rows, out_rows), :] for k in range(dff // out_rows)],
        out_stage, out_sems)

    @pl.when(first)
    def _():
        w_in.prime()
        w_out.prime()
        for k in range(d // in_rows):
            w_in.take(k, win16.at[pl.ds(k * in_rows, in_rows), :])

    x = x_ref[...]
    h = _rms_norm(x, gn_ref[...]).astype(_BF16)

    ss = jnp.zeros((tm, 1), _F32)
    for g in range(n_groups):
        z = _gelu(_dot(h, win16[:, dff + g * gdim: dff + (g + 1) * gdim]))
        v_scr[:, g * gdim:(g + 1) * gdim] = z
        ss = ss + jnp.sum(z * z, axis=-1, keepdims=True)
    r = lax.rsqrt(ss * (1.0 / dff) + EPS)

    row = lax.broadcasted_iota(jnp.int32, (A_CHUNK, A_CHUNK), 0) // CHUNK
    col = lax.broadcasted_iota(jnp.int32, (A_CHUNK, A_CHUNK), 1) // CHUNK
    causal = row >= col

    for g in range(n_groups):
        cols = slice(g * gdim, (g + 1) * gdim)
        vn = ((v_scr[:, cols] * r) * gsgu_ref[:, cols]).astype(_BF16)
        w = jnp.where(causal, ws_ref[g], 0.0).astype(_BF16)
        bias = bs_ref[:, g:g + 1]
        sp = jnp.concatenate(
            [_dot(w, vn[k * A_CHUNK:(k + 1) * A_CHUNK]) + bias for k in range(tm // A_CHUNK)],
            axis=0)
        u = _gelu(_dot(h, win16[:, cols]))
        t_scr[:, cols] = (u * sp).astype(_BF16)

    @pl.when(first)
    def _():
        for k in range(dff // out_rows):
            w_out.take(k, wout16.at[pl.ds(k * out_rows, out_rows), :])

    o_ref[...] = x + _dot(t_scr[...], wout16[...])


def _gmlp_layer(x2, layer, g_norm, w_in, g_sgu, w_s, b_s_t, w_out):
    t, d = x2.shape
    n_groups = w_s.shape[1]
    dff = w_out.shape[1]
    gdim = dff // n_groups
    tm = TOKEN_TILE
    in_rows, out_rows = GMLP_IN_STAGE_ROWS, GMLP_OUT_STAGE_ROWS
    assert d % in_rows == 0 and dff % out_rows == 0
    kern = functools.partial(_gmlp_kernel, layer=layer, n_groups=n_groups, gdim=gdim)
    return pl.pallas_call(
        kern,
        grid=(t // tm,),
        in_specs=[
            pl.BlockSpec((tm, d), lambda i: (i, 0)),
            _layer_spec(g_norm.shape, layer),
            pl.BlockSpec(memory_space=pl.ANY),
            _layer_spec(g_sgu.shape, layer),
            _layer_spec(w_s.shape, layer),
            _layer_spec(b_s_t.shape, layer),
            pl.BlockSpec(memory_space=pl.ANY),
        ],
        out_specs=pl.BlockSpec((tm, d), lambda i: (i, 0)),
        out_shape=jax.ShapeDtypeStruct((t, d), _F32),
        scratch_shapes=[pltpu.VMEM((d, 2 * dff), _BF16), pltpu.VMEM((dff, d), _BF16),
                        pltpu.VMEM((tm, dff), _F32), pltpu.VMEM((tm, dff), _BF16),
                        pltpu.VMEM((GMLP_STAGE_DEPTH, in_rows, 2 * dff), _F32),
                        pltpu.VMEM((GMLP_STAGE_DEPTH, out_rows, d), _F32),
                        pltpu.SemaphoreType.DMA((GMLP_STAGE_DEPTH,)),
                        pltpu.SemaphoreType.DMA((GMLP_STAGE_DEPTH,))],
        compiler_params=pltpu.CompilerParams(
            dimension_semantics=("arbitrary",), vmem_limit_bytes=V7X_VMEM_LIMIT_BYTES),
        name="gmlp_mixer",
    )(x2, g_norm, w_in, g_sgu, w_s, b_s_t, w_out)


def _ffn_kernel(x_ref, gn_ref, wgu_hbm, wd_hbm, fn_ref, o_ref,
                wgu16, wd16, a_scr, gu_stage, d_stage, gu_sems, d_sems,
                *, layer, hidden, final_norm):
    ck = V7X_MXU_DIM
    n_chunks = hidden // ck

    def gate_cols(c):
        return slice(c * ck, (c + 1) * ck)

    def up_cols(c):
        return slice(hidden + c * ck, hidden + (c + 1) * ck)

    def body(load_weights):
        if load_weights:
            gu_chunks = [cols(c) for c in range(n_chunks) for cols in (gate_cols, up_cols)]
            w_gu = _WeightStream([wgu_hbm.at[layer, :, c] for c in gu_chunks], gu_stage, gu_sems)
            w_d = _WeightStream([wd_hbm.at[layer, gate_cols(c), :] for c in range(n_chunks)],
                                d_stage, d_sems)
            w_gu.prime()
            w_d.prime()

        x = x_ref[...]
        h = _rms_norm(x, gn_ref[...]).astype(_BF16)
        for c in range(n_chunks):
            if load_weights:
                w_gu.take(2 * c, wgu16.at[:, gate_cols(c)])
                w_gu.take(2 * c + 1, wgu16.at[:, up_cols(c)])
                w_d.take(c, wd16.at[gate_cols(c), :])
            gate = _dot(h, wgu16[:, gate_cols(c)])
            up = _dot(h, wgu16[:, up_cols(c)])
            a_scr[:, gate_cols(c)] = ((gate * jax.nn.sigmoid(gate)) * up).astype(_BF16)
        y = x + _dot(a_scr[...], wd16[...])
        if final_norm:
            y = _rms_norm(y, fn_ref[...])
        o_ref[...] = y

    first = pl.program_id(0) == 0
    pl.when(first)(functools.partial(body, True))
    pl.when(jnp.logical_not(first))(functools.partial(body, False))


def _ffn_layer(x2, layer, g_norm, w_gate_up, w_down, g_final, final_norm):
    t, d = x2.shape
    hidden = w_down.shape[1]
    ck = V7X_MXU_DIM
    assert hidden % ck == 0
    tm = TOKEN_TILE
    kern = functools.partial(_ffn_kernel, layer=layer, hidden=hidden, final_norm=final_norm)
    return pl.pallas_call(
        kern,
        grid=(t // tm,),
        in_specs=[
            pl.BlockSpec((tm, d), lambda i: (i, 0)),
            _layer_spec(g_norm.shape, layer),
            pl.BlockSpec(memory_space=pl.ANY),
            pl.BlockSpec(memory_space=pl.ANY),
            _const_spec((1, d)),
        ],
        out_specs=pl.BlockSpec((tm, d), lambda i: (i, 0)),
        out_shape=jax.ShapeDtypeStruct((t, d), _F32),
        scratch_shapes=[pltpu.VMEM((d, 2 * hidden), _BF16), pltpu.VMEM((hidden, d), _BF16),
                        pltpu.VMEM((tm, hidden), _BF16),
                        pltpu.VMEM((2 * STAGE_DEPTH, d, ck), _F32),
                        pltpu.VMEM((STAGE_DEPTH, ck, d), _F32),
                        pltpu.SemaphoreType.DMA((2 * STAGE_DEPTH,)),
                        pltpu.SemaphoreType.DMA((STAGE_DEPTH,))],
        compiler_params=pltpu.CompilerParams(
            dimension_semantics=("arbitrary",), vmem_limit_bytes=V7X_VMEM_LIMIT_BYTES),
        name="swiglu_ffn",
    )(x2, g_norm, w_gate_up, w_down, g_final)


def _kv_kernel(x_ref, gn_ref, w_ref, k_ref, v_ref):
    j = pl.program_id(1)
    d = x_ref.shape[-1]

    @pl.when(j == 0)
    def _():
        k_ref[...] = jnp.zeros_like(k_ref)
        v_ref[...] = jnp.zeros_like(v_ref)

    @pl.when(j > 0)
    def _():
        h = _rms_norm(x_ref[0], gn_ref[...]).astype(_BF16)
        kv = _dot(h, w_ref[...])
        k_ref[0] = kv[:, :d].astype(_BF16)
        v_ref[0] = kv[:, d:].astype(_BF16)


def _shared_kv(x3, g_norm, w_kv):
    b, s, d = x3.shape
    pad = N_LEFT_CHUNKS * CHUNK
    tk = pad
    assert s % tk == 0
    out = jax.ShapeDtypeStruct((b, s + pad, d), _BF16)
    return pl.pallas_call(
        _kv_kernel,
        grid=(b, 1 + s // tk),
        in_specs=[
            pl.BlockSpec((1, tk, d), lambda bi, j: (bi, jnp.maximum(j - 1, 0), 0)),
            _const_spec((1, d)),
            _const_spec((d, 2 * d)),
        ],
        out_specs=[pl.BlockSpec((1, tk, d), lambda bi, j: (bi, j, 0))] * 2,
        out_shape=[out, out],
        compiler_params=pltpu.CompilerParams(
            dimension_semantics=("arbitrary", "arbitrary"),
            vmem_limit_bytes=V7X_VMEM_LIMIT_BYTES),
        name="shared_kv",
    )(x3, g_norm.reshape(1, d), w_kv.astype(_BF16))


def _bias_kernel(r_ref, o_ref):
    _, rows, width = o_ref.shape
    qc = lax.broadcasted_iota(jnp.int32, (1, width), 1) // CHUNK
    for r in range(rows):
        j = r // CHUNK
        in_band = (qc >= j) & (qc <= j + N_LEFT_CHUNKS)
        off = rows - 1 - r
        o_ref[:, r, :] = jnp.where(in_band, r_ref[:, off:off + width] * LOG2E, NEG_INF)


def _group_bias(rel_table, group_chunks):
    heads = rel_table.shape[0]
    rows = group_chunks * CHUNK
    width = (group_chunks + N_LEFT_CHUNKS) * CHUNK
    dist = (rows - 1 + N_LEFT_CHUNKS * CHUNK) - jnp.arange(rows - 1 + width)
    rev = rel_table[:, jnp.clip(dist, -MAX_REL, MAX_REL) + MAX_REL]
    return pl.pallas_call(
        _bias_kernel,
        out_shape=jax.ShapeDtypeStruct((heads, rows, width), _F32),
        name="rel_bias",
    )(rev)


def _attn_kernel(x_ref, gn_ref, wq_ref, klo_ref, khi_ref, vlo_ref, vhi_ref, bias_ref, wo_ref,
                 o_ref, q_scr, a_scr, k_scr, v_scr, s_scr, p_scr,
                 *, heads, head_dim, group_chunks):
    tq = x_ref.shape[1]
    rows = group_chunks * CHUNK
    width = (group_chunks + N_LEFT_CHUNKS) * CHUNK
    pad = N_LEFT_CHUNKS * CHUNK
    pair = V7X_LANES // head_dim
    n_slots = s_scr.shape[0]
    i = pl.program_id(1)

    k_scr[:tq] = klo_ref[0]
    k_scr[tq:] = khi_ref[0]
    v_scr[:tq] = vlo_ref[0]
    v_scr[tq:] = vhi_ref[0]

    x = x_ref[0]
    h = _rms_norm(x, gn_ref[...]).astype(_BF16)
    q_scr[...] = (_dot(h, wq_ref[...]) * (head_dim ** -0.5 * LOG2E)).astype(_BF16)

    lane = lax.broadcasted_iota(jnp.int32, (1, V7X_LANES), 1)
    kcol = lax.broadcasted_iota(jnp.int32, (1, width), 1)

    def group_body(gi, carry, *, mask_pad):
        r0 = pl.multiple_of(gi * rows, rows)
        valid = (i * tq + r0 + kcol) >= pad
        denom = {}

        def lanes_of(slab):
            return slice(slab * V7X_LANES, (slab + 1) * V7X_LANES)

        def mine_of(sub):
            return (lane >= sub * head_dim) & (lane < (sub + 1) * head_dim)

        def scores(slab):
            q2 = q_scr[pl.ds(r0, rows), lanes_of(slab)]
            qm = jnp.concatenate(
                [jnp.where(mine_of(sub), q2, jnp.zeros_like(q2)) for sub in range(pair)], axis=0)
            k2 = k_scr[pl.ds(r0, width), lanes_of(slab)]
            s = lax.dot_general(qm, k2, (((1,), (1,)), ((), ())),
                                preferred_element_type=_F32)
            s = s + bias_ref[slab]
            if mask_pad:
                s = jnp.where(valid, s, NEG_INF)
            s_scr[slab % n_slots] = s

        def softmax(slab):
            s = s_scr[slab % n_slots]
            p = jnp.exp2(s - jnp.max(s, axis=-1, keepdims=True))
            denom[slab] = jnp.sum(p, axis=-1, keepdims=True)
            p_scr[slab % n_slots] = p.astype(_BF16)

        def weighted_values(slab):
            v2 = v_scr[pl.ds(r0, width), lanes_of(slab)]
            o = _dot(p_scr[slab % n_slots], v2) * (1.0 / denom.pop(slab))
            out = o[:rows]
            for sub in range(1, pair):
                out = jnp.where(mine_of(sub), o[sub * rows:(sub + 1) * rows], out)
            a_scr[pl.ds(r0, rows), lanes_of(slab)] = out.astype(_BF16)

        n_slabs = heads // pair
        for t in range(n_slabs + 2):
            if t < n_slabs:
                scores(t)
            if 1 <= t <= n_slabs:
                softmax(t - 1)
            if t >= 2:
                weighted_values(t - 2)
        return carry

    n_groups = tq // rows

    @pl.when(i * tq < pad)
    def _():
        lax.fori_loop(0, n_groups, functools.partial(group_body, mask_pad=True), 0)

    @pl.when(i * tq >= pad)
    def _():
        lax.fori_loop(0, n_groups, functools.partial(group_body, mask_pad=False), 0)

    o_ref[0] = x + _dot(a_scr[...], wo_ref[...])


def _attn_layer(x3, layer, g_norm, w_q, bias, w_o, k_pad, v_pad, heads):
    b, s, d = x3.shape
    head_dim = d // heads
    tq = ATTN_TILE
    rows = Q_GROUP_CHUNKS * CHUNK
    width = (Q_GROUP_CHUNKS + N_LEFT_CHUNKS) * CHUNK
    pair = V7X_LANES // head_dim
    assert N_LEFT_CHUNKS * CHUNK <= tq
    bias = bias.reshape(heads // pair, pair * rows, width)
    kern = functools.partial(_attn_kernel, heads=heads, head_dim=head_dim,
                             group_chunks=Q_GROUP_CHUNKS)
    lo_spec = pl.BlockSpec((1, tq, d), lambda bi, i: (bi, i, 0))
    hi_spec = pl.BlockSpec((1, tq, d), lambda bi, i: (bi, i + 1, 0))
    return pl.pallas_call(
        kern,
        grid=(b, s // tq),
        in_specs=[
            pl.BlockSpec((1, tq, d), lambda bi, i: (bi, i, 0)),
            _layer_spec(g_norm.shape, layer),
            _layer_spec(w_q.shape, layer),
            lo_spec, hi_spec, lo_spec, hi_spec,
            _const_spec(bias.shape),
            _layer_spec(w_o.shape, layer),
        ],
        out_specs=pl.BlockSpec((1, tq, d), lambda bi, i: (bi, i, 0)),
        out_shape=jax.ShapeDtypeStruct((b, s, d), _F32),
        scratch_shapes=[pltpu.VMEM((tq, d), _BF16), pltpu.VMEM((tq, d), _BF16),
                        pltpu.VMEM((2 * tq, d), _BF16), pltpu.VMEM((2 * tq, d), _BF16),
                        pltpu.VMEM((ATTN_SLOTS, pair * rows, width), _F32),
                        pltpu.VMEM((ATTN_SLOTS, pair * rows, width), _BF16)],
        compiler_params=pltpu.CompilerParams(
            dimension_semantics=("arbitrary", "arbitrary"),
            vmem_limit_bytes=V7X_VMEM_LIMIT_BYTES),
        name="banded_attention",
    )(x3, g_norm, w_q, k_pad, k_pad, v_pad, v_pad, bias, w_o)


def kernel(x, a_norm, a_w_in, a_sgu_norm, a_w_spatial, a_b_spatial, a_w_out, kv_norm, w_kv,
           b_norm, b_w_q, b_rel_bias, b_w_o, ffn_norm, ffn_w_gate_up, ffn_w_down, final_norm):
    b, s, d = x.shape
    n_a = a_norm.shape[0]
    depth = ffn_norm.shape[0]
    heads = b_rel_bias.shape[1]
    assert s % ATTN_TILE == 0 and (b * s) % TOKEN_TILE == 0
    assert ATTN_TILE % (Q_GROUP_CHUNKS * CHUNK) == 0 and TOKEN_TILE % A_CHUNK == 0

    a_norm3 = a_norm[:, None, :]
    a_sgu3 = a_sgu_norm[:, None, :]
    a_bs_t = jnp.swapaxes(a_b_spatial, 1, 2)
    b_norm3 = b_norm[:, None, :]
    b_w_q16, b_w_o16 = b_w_q.astype(_BF16), b_w_o.astype(_BF16)
    ffn_norm3 = ffn_norm[:, None, :]
    g_final = final_norm.reshape(1, d)

    x2 = x.reshape(b * s, d)
    k_pad = v_pad = None
    for layer in range(depth):
        if layer < n_a:
            x2 = _gmlp_layer(x2, layer, a_norm3, a_w_in, a_sgu3, a_w_spatial, a_bs_t, a_w_out)
        else:
            x3 = x2.reshape(b, s, d)
            if layer == n_a:
                k_pad, v_pad = _shared_kv(x3, kv_norm, w_kv)
            li = layer - n_a
            bias = _group_bias(b_rel_bias[li], Q_GROUP_CHUNKS)
            x2 = _attn_layer(x3, li, b_norm3, b_w_q16, bias, b_w_o16, k_pad, v_pad,
                             heads).reshape(b * s, d)
        x2 = _ffn_layer(x2, layer, ffn_norm3, ffn_w_gate_up, ffn_w_down, g_final,
                        final_norm=(layer == depth - 1))
    return x2.reshape(b, s, d)
```

```python
import functools
import math

import jax
import jax.numpy as jnp
from jax import lax
from jax.experimental import pallas as pl
from jax.experimental.pallas import tpu as pltpu

EPS = 1e-6
CHUNK = 64
A_CHUNK = 128
N_LEFT_CHUNKS = 8
MAX_REL = 256
NEG_INF = -1e30
LOG2E = math.log2(math.e)

V7X_LANES = 128
V7X_MXU_DIM = 256
V7X_VMEM_LIMIT_BYTES = 56 * 1024 * 1024

TOKEN_TILE = 512
Q_GROUP_CHUNKS = 2
ATTN_TILE = 512
ATTN_SLOTS = 4
STAGE_DEPTH = 3
PITCH_PAD = V7X_LANES

_BF16 = jnp.bfloat16
_F32 = jnp.float32


def _const_spec(shape):
    nd = len(shape)
    return pl.BlockSpec(shape, lambda *_: (0,) * nd, pipeline_mode=pl.Buffered(1))


def _layer_spec(stacked_shape, layer):
    nd = len(stacked_shape)
    return pl.BlockSpec((None,) + tuple(stacked_shape[1:]),
                        lambda *_: (layer,) + (0,) * (nd - 1), pipeline_mode=pl.Buffered(1))


def _rms_norm(x, g):
    return (x * lax.rsqrt(jnp.mean(x * x, axis=-1, keepdims=True) + EPS)) * g


def _dot(a, b):
    return jnp.dot(a, b, preferred_element_type=_F32)


def _gelu(x):
    return 0.5 * x * (1.0 + lax.erf(x * (1.0 / math.sqrt(2.0))))


class _WeightStream:
    def __init__(self, sources, stage, sems):
        self.sources, self.stage, self.sems = sources, stage, sems
        self.depth = stage.shape[0]

    def _copy(self, k):
        slot = k % self.depth
        return pltpu.make_async_copy(self.sources[k], self.stage.at[slot], self.sems.at[slot])

    def prime(self):
        for k in range(min(self.depth, len(self.sources))):
            self._copy(k).start()

    def take(self, k, dst):
        self._copy(k).wait()
        dst[...] = self.stage[k % self.depth].astype(_BF16)
        if k + self.depth < len(self.sources):
            self._copy(k + self.depth).start()


def _gmlp_kernel(x_ref, gn_ref, win_hbm, gsgu_ref, ws_ref, bs_ref, wout_hbm, o_ref,
                 win16, wout16, v_scr, t_scr, in_stage, out_stage, in_sems, out_sems,
                 *, layer, n_groups, gdim):
    tm, d = x_ref.shape
    dff = n_groups * gdim

    def gate_cols(g):
        return slice(dff + g * gdim, dff + (g + 1) * gdim)

    def lin_cols(g):
        return slice(g * gdim, (g + 1) * gdim)

    def body(load_weights):
        if load_weights:
            in_chunks = [gate_cols(g) for g in range(n_groups)] + [lin_cols(g) for g in range(n_groups)]
            w_in = _WeightStream([win_hbm.at[layer, :, c] for c in in_chunks], in_stage, in_sems)
            w_out = _WeightStream([wout_hbm.at[layer, lin_cols(g), :] for g in range(n_groups)],
                                  out_stage, out_sems)
            w_in.prime()
            w_out.prime()

        x = x_ref[...]
        h = _rms_norm(x, gn_ref[...]).astype(_BF16)

        ss = jnp.zeros((tm, 1), _F32)
        for g in range(n_groups):
            if load_weights:
                w_in.take(g, win16.at[:, gate_cols(g)])
            z = _gelu(_dot(h, win16[:, gate_cols(g)]))
            v_scr[:, lin_cols(g)] = z
            ss = ss + jnp.sum(z * z, axis=-1, keepdims=True)
        r = lax.rsqrt(ss * (1.0 / dff) + EPS)

        row = lax.broadcasted_iota(jnp.int32, (A_CHUNK, A_CHUNK), 0) // CHUNK
        col = lax.broadcasted_iota(jnp.int32, (A_CHUNK, A_CHUNK), 1) // CHUNK
        causal = row >= col

        for g in range(n_groups):
            cols = lin_cols(g)
            if load_weights:
                w_in.take(n_groups + g, win16.at[:, cols])
                w_out.take(g, wout16.at[cols, :d])
            vn = ((v_scr[:, cols] * r) * gsgu_ref[:, cols]).astype(_BF16)
            w = jnp.where(causal, ws_ref[g], 0.0).astype(_BF16)
            bias = bs_ref[:, g:g + 1]
            sp = jnp.concatenate(
                [_dot(w, vn[k * A_CHUNK:(k + 1) * A_CHUNK]) + bias for k in range(tm // A_CHUNK)],
                axis=0)
            u = _gelu(_dot(h, win16[:, cols]))
            t_scr[:, cols] = (u * sp).astype(_BF16)

        o_ref[...] = x + _dot(t_scr[...], wout16[:, :d])

    first = pl.program_id(0) == 0
    pl.when(first)(functools.partial(body, True))
    pl.when(jnp.logical_not(first))(functools.partial(body, False))


def _gmlp_layer(x2, layer, g_norm, w_in, g_sgu, w_s, b_s_t, w_out):
    t, d = x2.shape
    n_groups = w_s.shape[1]
    dff = w_out.shape[1]
    gdim = dff // n_groups
    tm = TOKEN_TILE
    kern = functools.partial(_gmlp_kernel, layer=layer, n_groups=n_groups, gdim=gdim)
    return pl.pallas_call(
        kern,
        grid=(t // tm,),
        in_specs=[
            pl.BlockSpec((tm, d), lambda i: (i, 0)),
            _layer_spec(g_norm.shape, layer),
            pl.BlockSpec(memory_space=pl.ANY),
            _layer_spec(g_sgu.shape, layer),
            _layer_spec(w_s.shape, layer),
            _layer_spec(b_s_t.shape, layer),
            pl.BlockSpec(memory_space=pl.ANY),
        ],
        out_specs=pl.BlockSpec((tm, d), lambda i: (i, 0)),
        out_shape=jax.ShapeDtypeStruct((t, d), _F32),
        scratch_shapes=[pltpu.VMEM((d, 2 * dff + PITCH_PAD), _BF16),
                        pltpu.VMEM((dff, d + PITCH_PAD), _BF16),
                        pltpu.VMEM((tm, dff), _F32), pltpu.VMEM((tm, dff), _BF16),
                        pltpu.VMEM((2 * STAGE_DEPTH, d, gdim), _F32),
                        pltpu.VMEM((STAGE_DEPTH, gdim, d), _F32),
                        pltpu.SemaphoreType.DMA((2 * STAGE_DEPTH,)),
                        pltpu.SemaphoreType.DMA((STAGE_DEPTH,))],
        compiler_params=pltpu.CompilerParams(
            dimension_semantics=("arbitrary",), vmem_limit_bytes=V7X_VMEM_LIMIT_BYTES),
        name="gmlp_mixer",
    )(x2, g_norm, w_in, g_sgu, w_s, b_s_t, w_out)


def _ffn_kernel(x_ref, gn_ref, wgu_hbm, wd_hbm, fn_ref, o_ref,
                wgu16, wd16, a_scr, gu_stage, d_stage, gu_sems, d_sems,
                *, layer, hidden, final_norm):
    ck = V7X_MXU_DIM
    n_chunks = hidden // ck

    def gate_cols(c):
        return slice(c * ck, (c + 1) * ck)

    def up_cols(c):
        return slice(hidden + c * ck, hidden + (c + 1) * ck)

    def body(load_weights):
        if load_weights:
            gu_chunks = [cols(c) for c in range(n_chunks) for cols in (gate_cols, up_cols)]
            w_gu = _WeightStream([wgu_hbm.at[layer, :, c] for c in gu_chunks], gu_stage, gu_sems)
            w_d = _WeightStream([wd_hbm.at[layer, gate_cols(c), :] for c in range(n_chunks)],
                                d_stage, d_sems)
            w_gu.prime()
            w_d.prime()

        x = x_ref[...]
        h = _rms_norm(x, gn_ref[...]).astype(_BF16)
        for c in range(n_chunks):
            if load_weights:
                w_gu.take(2 * c, wgu16.at[:, gate_cols(c)])
                w_gu.take(2 * c + 1, wgu16.at[:, up_cols(c)])
                w_d.take(c, wd16.at[gate_cols(c), :])
            gate = _dot(h, wgu16[:, gate_cols(c)])
            up = _dot(h, wgu16[:, up_cols(c)])
            a_scr[:, gate_cols(c)] = ((gate * jax.nn.sigmoid(gate)) * up).astype(_BF16)
        y = x + _dot(a_scr[...], wd16[...])
        if final_norm:
            y = _rms_norm(y, fn_ref[...])
        o_ref[...] = y

    first = pl.program_id(0) == 0
    pl.when(first)(functools.partial(body, True))
    pl.when(jnp.logical_not(first))(functools.partial(body, False))


def _ffn_layer(x2, layer, g_norm, w_gate_up, w_down, g_final, final_norm):
    t, d = x2.shape
    hidden = w_down.shape[1]
    ck = V7X_MXU_DIM
    assert hidden % ck == 0
    tm = TOKEN_TILE
    kern = functools.partial(_ffn_kernel, layer=layer, hidden=hidden, final_norm=final_norm)
    return pl.pallas_call(
        kern,
        grid=(t // tm,),
        in_specs=[
            pl.BlockSpec((tm, d), lambda i: (i, 0)),
            _layer_spec(g_norm.shape, layer),
            pl.BlockSpec(memory_space=pl.ANY),
            pl.BlockSpec(memory_space=pl.ANY),
            _const_spec((1, d)),
        ],
        out_specs=pl.BlockSpec((tm, d), lambda i: (i, 0)),
        out_shape=jax.ShapeDtypeStruct((t, d), _F32),
        scratch_shapes=[pltpu.VMEM((d, 2 * hidden), _BF16), pltpu.VMEM((hidden, d), _BF16),
                        pltpu.VMEM((tm, hidden), _BF16),
                        pltpu.VMEM((2 * STAGE_DEPTH, d, ck), _F32),
                        pltpu.VMEM((STAGE_DEPTH, ck, d), _F32),
                        pltpu.SemaphoreType.DMA((2 * STAGE_DEPTH,)),
                        pltpu.SemaphoreType.DMA((STAGE_DEPTH,))],
        compiler_params=pltpu.CompilerParams(
            dimension_semantics=("arbitrary",), vmem_limit_bytes=V7X_VMEM_LIMIT_BYTES),
        name="swiglu_ffn",
    )(x2, g_norm, w_gate_up, w_down, g_final)


def _kv_kernel(x_ref, gn_ref, w_ref, k_ref, v_ref):
    j = pl.program_id(1)
    d = x_ref.shape[-1]

    @pl.when(j == 0)
    def _():
        k_ref[...] = jnp.zeros_like(k_ref)
        v_ref[...] = jnp.zeros_like(v_ref)

    @pl.when(j > 0)
    def _():
        h = _rms_norm(x_ref[0], gn_ref[...]).astype(_BF16)
        kv = _dot(h, w_ref[...])
        k_ref[0] = kv[:, :d].astype(_BF16)
        v_ref[0] = kv[:, d:].astype(_BF16)


def _shared_kv(x3, g_norm, w_kv):
    b, s, d = x3.shape
    pad = N_LEFT_CHUNKS * CHUNK
    tk = pad
    assert s % tk == 0
    out = jax.ShapeDtypeStruct((b, s + pad, d), _BF16)
    return pl.pallas_call(
        _kv_kernel,
        grid=(b, 1 + s // tk),
        in_specs=[
            pl.BlockSpec((1, tk, d), lambda bi, j: (bi, jnp.maximum(j - 1, 0), 0)),
            _const_spec((1, d)),
            _const_spec((d, 2 * d)),
        ],
        out_specs=[pl.BlockSpec((1, tk, d), lambda bi, j: (bi, j, 0))] * 2,
        out_shape=[out, out],
        compiler_params=pltpu.CompilerParams(
            dimension_semantics=("arbitrary", "arbitrary"),
            vmem_limit_bytes=V7X_VMEM_LIMIT_BYTES),
        name="shared_kv",
    )(x3, g_norm.reshape(1, d), w_kv.astype(_BF16))


def _bias_kernel(r_ref, o_ref):
    _, rows, width = o_ref.shape
    qc = lax.broadcasted_iota(jnp.int32, (1, width), 1) // CHUNK
    for r in range(rows):
        j = r // CHUNK
        in_band = (qc >= j) & (qc <= j + N_LEFT_CHUNKS)
        off = rows - 1 - r
        o_ref[:, r, :] = jnp.where(in_band, r_ref[:, off:off + width] * LOG2E, NEG_INF)


def _group_bias(rel_table, group_chunks):
    heads = rel_table.shape[0]
    rows = group_chunks * CHUNK
    width = (group_chunks + N_LEFT_CHUNKS) * CHUNK
    dist = (rows - 1 + N_LEFT_CHUNKS * CHUNK) - jnp.arange(rows - 1 + width)
    rev = rel_table[:, jnp.clip(dist, -MAX_REL, MAX_REL) + MAX_REL]
    return pl.pallas_call(
        _bias_kernel,
        out_shape=jax.ShapeDtypeStruct((heads, rows, width), _F32),
        name="rel_bias",
    )(rev)


def _attn_kernel(x_ref, gn_ref, wq_ref, klo_ref, khi_ref, vlo_ref, vhi_ref, bias_ref, wo_ref,
                 o_ref, q_scr, a_scr, k_scr, v_scr, s_scr, p_scr,
                 *, heads, head_dim, group_chunks):
    tq = x_ref.shape[1]
    rows = group_chunks * CHUNK
    width = (group_chunks + N_LEFT_CHUNKS) * CHUNK
    pad = N_LEFT_CHUNKS * CHUNK
    pair = V7X_LANES // head_dim
    n_slots = s_scr.shape[0]
    i = pl.program_id(1)

    d = x_ref.shape[2]
    k_scr[:tq, :d] = klo_ref[0]
    k_scr[tq:, :d] = khi_ref[0]
    v_scr[:tq, :d] = vlo_ref[0]
    v_scr[tq:, :d] = vhi_ref[0]

    x = x_ref[0]
    h = _rms_norm(x, gn_ref[...]).astype(_BF16)
    q_scr[...] = (_dot(h, wq_ref[...]) * (head_dim ** -0.5 * LOG2E)).astype(_BF16)

    lane = lax.broadcasted_iota(jnp.int32, (1, V7X_LANES), 1)
    kcol = lax.broadcasted_iota(jnp.int32, (1, width), 1)

    def group_body(gi, carry, *, mask_pad):
        r0 = pl.multiple_of(gi * rows, rows)
        valid = (i * tq + r0 + kcol) >= pad
        denom = {}

        def lanes_of(slab):
            return slice(slab * V7X_LANES, (slab + 1) * V7X_LANES)

        def mine_of(sub):
            return (lane >= sub * head_dim) & (lane < (sub + 1) * head_dim)

        def scores(slab):
            q2 = q_scr[pl.ds(r0, rows), lanes_of(slab)]
            qm = jnp.concatenate(
                [jnp.where(mine_of(sub), q2, jnp.zeros_like(q2)) for sub in range(pair)], axis=0)
            k2 = k_scr[pl.ds(r0, width), lanes_of(slab)]
            s = lax.dot_general(qm, k2, (((1,), (1,)), ((), ())),
                                preferred_element_type=_F32)
            s = s + bias_ref[slab]
            if mask_pad:
                s = jnp.where(valid, s, NEG_INF)
            s_scr[slab % n_slots] = s

        def softmax(slab):
            s = s_scr[slab % n_slots]
            p = jnp.exp2(s - jnp.max(s, axis=-1, keepdims=True))
            denom[slab] = jnp.sum(p, axis=-1, keepdims=True)
            p_scr[slab % n_slots] = p.astype(_BF16)

        def weighted_values(slab):
            v2 = v_scr[pl.ds(r0, width), lanes_of(slab)]
            o = _dot(p_scr[slab % n_slots], v2) * (1.0 / denom.pop(slab))
            out = o[:rows]
            for sub in range(1, pair):
                out = jnp.where(mine_of(sub), o[sub * rows:(sub + 1) * rows], out)
            a_scr[pl.ds(r0, rows), lanes_of(slab)] = out.astype(_BF16)

        n_slabs = heads // pair
        for t in range(n_slabs + 2):
            if t < n_slabs:
                scores(t)
            if 1 <= t <= n_slabs:
                softmax(t - 1)
            if t >= 2:
                weighted_values(t - 2)
        return carry

    n_groups = tq // rows

    @pl.when(i * tq < pad)
    def _():
        lax.fori_loop(0, n_groups, functools.partial(group_body, mask_pad=True), 0)

    @pl.when(i * tq >= pad)
    def _():
        lax.fori_loop(0, n_groups, functools.partial(group_body, mask_pad=False), 0)

    o_ref[0] = x + _dot(a_scr[...], wo_ref[...])


def _attn_layer(x3, layer, g_norm, w_q, bias, w_o, k_pad, v_pad, heads):
    b, s, d = x3.shape
    head_dim = d // heads
    tq = ATTN_TILE
    rows = Q_GROUP_CHUNKS * CHUNK
    width = (Q_GROUP_CHUNKS + N_LEFT_CHUNKS) * CHUNK
    pair = V7X_LANES // head_dim
    assert N_LEFT_CHUNKS * CHUNK <= tq
    bias = bias.reshape(heads // pair, pair * rows, width)
    kern = functools.partial(_attn_kernel, heads=heads, head_dim=head_dim,
                             group_chunks=Q_GROUP_CHUNKS)
    lo_spec = pl.BlockSpec((1, tq, d), lambda bi, i: (bi, i, 0))
    hi_spec = pl.BlockSpec((1, tq, d), lambda bi, i: (bi, i + 1, 0))
    return pl.pallas_call(
        kern,
        grid=(b, s // tq),
        in_specs=[
            pl.BlockSpec((1, tq, d), lambda bi, i: (bi, i, 0)),
            _layer_spec(g_norm.shape, layer),
            _layer_spec(w_q.shape, layer),
            lo_spec, hi_spec, lo_spec, hi_spec,
            _const_spec(bias.shape),
            _layer_spec(w_o.shape, layer),
        ],
        out_specs=pl.BlockSpec((1, tq, d), lambda bi, i: (bi, i, 0)),
        out_shape=jax.ShapeDtypeStruct((b, s, d), _F32),
        scratch_shapes=[pltpu.VMEM((tq, d), _BF16), pltpu.VMEM((tq, d), _BF16),
                        pltpu.VMEM((2 * tq, d + PITCH_PAD), _BF16),
                        pltpu.VMEM((2 * tq, d + PITCH_PAD), _BF16),
                        pltpu.VMEM((ATTN_SLOTS, pair * rows, width), _F32),
                        pltpu.VMEM((ATTN_SLOTS, pair * rows, width), _BF16)],
        compiler_params=pltpu.CompilerParams(
            dimension_semantics=("arbitrary", "arbitrary"),
            vmem_limit_bytes=V7X_VMEM_LIMIT_BYTES),
        name="banded_attention",
    )(x3, g_norm, w_q, k_pad, k_pad, v_pad, v_pad, bias, w_o)


def kernel(x, a_norm, a_w_in, a_sgu_norm, a_w_spatial, a_b_spatial, a_w_out, kv_norm, w_kv,
           b_norm, b_w_q, b_rel_bias, b_w_o, ffn_norm, ffn_w_gate_up, ffn_w_down, final_norm):
    b, s, d = x.shape
    n_a = a_norm.shape[0]
    depth = ffn_norm.shape[0]
    heads = b_rel_bias.shape[1]
    assert s % ATTN_TILE == 0 and (b * s) % TOKEN_TILE == 0
    assert ATTN_TILE % (Q_GROUP_CHUNKS * CHUNK) == 0 and TOKEN_TILE % A_CHUNK == 0

    a_norm3 = a_norm[:, None, :]
    a_sgu3 = a_sgu_norm[:, None, :]
    a_bs_t = jnp.swapaxes(a_b_spatial, 1, 2)
    b_norm3 = b_norm[:, None, :]
    b_w_q16, b_w_o16 = b_w_q.astype(_BF16), b_w_o.astype(_BF16)
    ffn_norm3 = ffn_norm[:, None, :]
    g_final = final_norm.reshape(1, d)

    x2 = x.reshape(b * s, d)
    k_pad = v_pad = None
    for layer in range(depth):
        if layer < n_a:
            x2 = _gmlp_layer(x2, layer, a_norm3, a_w_in, a_sgu3, a_w_spatial, a_bs_t, a_w_out)
        else:
            x3 = x2.reshape(b, s, d)
            if layer == n_a:
                k_pad, v_pad = _shared_kv(x3, kv_norm, w_kv)
            li = layer - n_a
            bias = _group_bias(b_rel_bias[li], Q_GROUP_CHUNKS)
            x2 = _attn_layer(x3, li, b_norm3, b_w_q16, bias, b_w_o16, k_pad, v_pad,
                             heads).reshape(b * s, d)
        x2 = _ffn_layer(x2, layer, ffn_norm3, ffn_w_gate_up, ffn_w_down, g_final,
                        final_norm=(layer == depth - 1))
    return x2.reshape(b, s, d)
```

```python
import functools
import math

import jax
import jax.numpy as jnp
from jax import lax
from jax.experimental import pallas as pl
from jax.experimental.pallas import tpu as pltpu

EPS = 1e-6
CHUNK = 64
A_CHUNK = 128
N_LEFT_CHUNKS = 8
MAX_REL = 256
NEG_INF = -1e30
LOG2E = math.log2(math.e)

V7X_LANES = 128
V7X_MXU_DIM = 256
V7X_VMEM_LIMIT_BYTES = 56 * 1024 * 1024

GMLP_TILE = 512
FFN_TILE = 1024
KV_TILE = 1024
SOFTMAX_LAG = 1
PV_LAG = 3
Q_GROUP_CHUNKS = 2
ATTN_TILE = 512
ATTN_SLOTS = 4
STAGE_DEPTH = 3
PITCH_PAD = V7X_LANES

_BF16 = jnp.bfloat16
_F32 = jnp.float32


def _const_spec(shape):
    nd = len(shape)
    return pl.BlockSpec(shape, lambda *_: (0,) * nd, pipeline_mode=pl.Buffered(1))


def _layer_spec(stacked_shape, layer):
    nd = len(stacked_shape)
    return pl.BlockSpec((None,) + tuple(stacked_shape[1:]),
                        lambda *_: (layer,) + (0,) * (nd - 1), pipeline_mode=pl.Buffered(1))


def _rms_norm(x, g):
    return (x * lax.rsqrt(jnp.mean(x * x, axis=-1, keepdims=True) + EPS)) * g


def _dot(a, b):
    return jnp.dot(a, b, preferred_element_type=_F32)


def _gelu(x):
    return 0.5 * x * (1.0 + lax.erf(x * (1.0 / math.sqrt(2.0))))


class _WeightStream:
    def __init__(self, sources, stage, sems):
        self.sources, self.stage, self.sems = sources, stage, sems
        self.depth = stage.shape[0]

    def _copy(self, k):
        slot = k % self.depth
        return pltpu.make_async_copy(self.sources[k], self.stage.at[slot], self.sems.at[slot])

    def prime(self):
        for k in range(min(self.depth, len(self.sources))):
            self._copy(k).start()

    def take(self, k, dst):
        self._copy(k).wait()
        dst[...] = self.stage[k % self.depth].astype(_BF16)
        if k + self.depth < len(self.sources):
            self._copy(k + self.depth).start()


def _gmlp_kernel(x_ref, gn_ref, win_hbm, gsgu_ref, ws_ref, bs_ref, wout_hbm, o_ref,
                 win16, wout16, v_scr, t_scr, in_stage, out_stage, in_sems, out_sems,
                 *, layer, n_groups, gdim):
    tm, d = x_ref.shape
    dff = n_groups * gdim

    def gate_cols(g):
        return slice(dff + g * gdim, dff + (g + 1) * gdim)

    def lin_cols(g):
        return slice(g * gdim, (g + 1) * gdim)

    def body(load_weights):
        if load_weights:
            in_chunks = [gate_cols(g) for g in range(n_groups)] + [lin_cols(g) for g in range(n_groups)]
            w_in = _WeightStream([win_hbm.at[layer, :, c] for c in in_chunks], in_stage, in_sems)
            w_out = _WeightStream([wout_hbm.at[layer, lin_cols(g), :] for g in range(n_groups)],
                                  out_stage, out_sems)
            w_in.prime()
            w_out.prime()

        x = x_ref[...]
        h = _rms_norm(x, gn_ref[...]).astype(_BF16)

        ss = jnp.zeros((tm, 1), _F32)
        for g in range(n_groups):
            if load_weights:
                w_in.take(g, win16.at[:, gate_cols(g)])
            z = _gelu(_dot(h, win16[:, gate_cols(g)]))
            v_scr[:, lin_cols(g)] = z
            ss = ss + jnp.sum(z * z, axis=-1, keepdims=True)
        r = lax.rsqrt(ss * (1.0 / dff) + EPS)

        row = lax.broadcasted_iota(jnp.int32, (A_CHUNK, A_CHUNK), 0) // CHUNK
        col = lax.broadcasted_iota(jnp.int32, (A_CHUNK, A_CHUNK), 1) // CHUNK
        causal = row >= col

        for g in range(n_groups):
            cols = lin_cols(g)
            if load_weights:
                w_in.take(n_groups + g, win16.at[:, cols])
                w_out.take(g, wout16.at[cols, :d])
            vn = ((v_scr[:, cols] * r) * gsgu_ref[:, cols]).astype(_BF16)
            w = jnp.where(causal, ws_ref[g], 0.0).astype(_BF16)
            bias = bs_ref[:, g:g + 1]
            sp = jnp.concatenate(
                [_dot(w, vn[k * A_CHUNK:(k + 1) * A_CHUNK]) + bias for k in range(tm // A_CHUNK)],
                axis=0)
            u = _gelu(_dot(h, win16[:, cols]))
            t_scr[:, cols] = (u * sp).astype(_BF16)

        o_ref[...] = x + _dot(t_scr[...], wout16[:, :d])

    first = pl.program_id(0) == 0
    pl.when(first)(functools.partial(body, True))
    pl.when(jnp.logical_not(first))(functools.partial(body, False))


def _gmlp_layer(x2, layer, g_norm, w_in, g_sgu, w_s, b_s_t, w_out):
    t, d = x2.shape
    n_groups = w_s.shape[1]
    dff = w_out.shape[1]
    gdim = dff // n_groups
    tm = GMLP_TILE
    kern = functools.partial(_gmlp_kernel, layer=layer, n_groups=n_groups, gdim=gdim)
    return pl.pallas_call(
        kern,
        grid=(t // tm,),
        in_specs=[
            pl.BlockSpec((tm, d), lambda i: (i, 0)),
            _layer_spec(g_norm.shape, layer),
            pl.BlockSpec(memory_space=pl.ANY),
            _layer_spec(g_sgu.shape, layer),
            _layer_spec(w_s.shape, layer),
            _layer_spec(b_s_t.shape, layer),
            pl.BlockSpec(memory_space=pl.ANY),
        ],
        out_specs=pl.BlockSpec((tm, d), lambda i: (i, 0)),
        out_shape=jax.ShapeDtypeStruct((t, d), _F32),
        scratch_shapes=[pltpu.VMEM((d, 2 * dff + PITCH_PAD), _BF16),
                        pltpu.VMEM((dff, d + PITCH_PAD), _BF16),
                        pltpu.VMEM((tm, dff), _F32), pltpu.VMEM((tm, dff), _BF16),
                        pltpu.VMEM((2 * STAGE_DEPTH, d, gdim), _F32),
                        pltpu.VMEM((STAGE_DEPTH, gdim, d), _F32),
                        pltpu.SemaphoreType.DMA((2 * STAGE_DEPTH,)),
                        pltpu.SemaphoreType.DMA((STAGE_DEPTH,))],
        compiler_params=pltpu.CompilerParams(
            dimension_semantics=("arbitrary",), vmem_limit_bytes=V7X_VMEM_LIMIT_BYTES),
        name="gmlp_mixer",
    )(x2, g_norm, w_in, g_sgu, w_s, b_s_t, w_out)


def _ffn_kernel(x_ref, gn_ref, wgu_hbm, wd_hbm, fn_ref, o_ref,
                wgu16, wd16, a_scr, gu_stage, d_stage, gu_sems, d_sems,
                *, layer, hidden, final_norm):
    ck = V7X_MXU_DIM
    n_chunks = hidden // ck

    def gate_cols(c):
        return slice(c * ck, (c + 1) * ck)

    def up_cols(c):
        return slice(hidden + c * ck, hidden + (c + 1) * ck)

    def body(load_weights):
        if load_weights:
            gu_chunks = [cols(c) for c in range(n_chunks) for cols in (gate_cols, up_cols)]
            w_gu = _WeightStream([wgu_hbm.at[layer, :, c] for c in gu_chunks], gu_stage, gu_sems)
            w_d = _WeightStream([wd_hbm.at[layer, gate_cols(c), :] for c in range(n_chunks)],
                                d_stage, d_sems)
            w_gu.prime()
            w_d.prime()

        x = x_ref[...]
        h = _rms_norm(x, gn_ref[...]).astype(_BF16)
        for c in range(n_chunks):
            if load_weights:
                w_gu.take(2 * c, wgu16.at[:, gate_cols(c)])
                w_gu.take(2 * c + 1, wgu16.at[:, up_cols(c)])
                w_d.take(c, wd16.at[gate_cols(c), :])
            gate = _dot(h, wgu16[:, gate_cols(c)])
            up = _dot(h, wgu16[:, up_cols(c)])
            a_scr[:, gate_cols(c)] = ((gate * jax.nn.sigmoid(gate)) * up).astype(_BF16)
        y = x + _dot(a_scr[...], wd16[...])
        if final_norm:
            y = _rms_norm(y, fn_ref[...])
        o_ref[...] = y

    first = pl.program_id(0) == 0
    pl.when(first)(functools.partial(body, True))
    pl.when(jnp.logical_not(first))(functools.partial(body, False))


def _ffn_layer(x2, layer, g_norm, w_gate_up, w_down, g_final, final_norm):
    t, d = x2.shape
    hidden = w_down.shape[1]
    ck = V7X_MXU_DIM
    assert hidden % ck == 0
    tm = FFN_TILE
    kern = functools.partial(_ffn_kernel, layer=layer, hidden=hidden, final_norm=final_norm)
    return pl.pallas_call(
        kern,
        grid=(t // tm,),
        in_specs=[
            pl.BlockSpec((tm, d), lambda i: (i, 0)),
            _layer_spec(g_norm.shape, layer),
            pl.BlockSpec(memory_space=pl.ANY),
            pl.BlockSpec(memory_space=pl.ANY),
            _const_spec((1, d)),
        ],
        out_specs=pl.BlockSpec((tm, d), lambda i: (i, 0)),
        out_shape=jax.ShapeDtypeStruct((t, d), _F32),
        scratch_shapes=[pltpu.VMEM((d, 2 * hidden), _BF16), pltpu.VMEM((hidden, d), _BF16),
                        pltpu.VMEM((tm, hidden), _BF16),
                        pltpu.VMEM((2 * STAGE_DEPTH, d, ck), _F32),
                        pltpu.VMEM((STAGE_DEPTH, ck, d), _F32),
                        pltpu.SemaphoreType.DMA((2 * STAGE_DEPTH,)),
                        pltpu.SemaphoreType.DMA((STAGE_DEPTH,))],
        compiler_params=pltpu.CompilerParams(
            dimension_semantics=("arbitrary",), vmem_limit_bytes=V7X_VMEM_LIMIT_BYTES),
        name="swiglu_ffn",
    )(x2, g_norm, w_gate_up, w_down, g_final)


def _kv_kernel(x_ref, gn_ref, w_ref, k_ref, v_ref):
    d = x_ref.shape[-1]
    h = _rms_norm(x_ref[...], gn_ref[...]).astype(_BF16)
    kv = _dot(h, w_ref[...])
    k_ref[...] = kv[:, :d].astype(_BF16)
    v_ref[...] = kv[:, d:].astype(_BF16)


def _shared_kv(x2, g_norm, w_kv):
    t, d = x2.shape
    tk = KV_TILE
    out = jax.ShapeDtypeStruct((t, d), _BF16)
    return pl.pallas_call(
        _kv_kernel,
        grid=(t // tk,),
        in_specs=[
            pl.BlockSpec((tk, d), lambda i: (i, 0)),
            _const_spec((1, d)),
            _const_spec((d, 2 * d)),
        ],
        out_specs=[pl.BlockSpec((tk, d), lambda i: (i, 0))] * 2,
        out_shape=[out, out],
        compiler_params=pltpu.CompilerParams(
            dimension_semantics=("arbitrary",), vmem_limit_bytes=V7X_VMEM_LIMIT_BYTES),
        name="shared_kv",
    )(x2, g_norm.reshape(1, d), w_kv.astype(_BF16))


def _bias_kernel(r_ref, o_ref):
    _, rows, width = o_ref.shape
    qc = lax.broadcasted_iota(jnp.int32, (1, width), 1) // CHUNK
    for r in range(rows):
        j = r // CHUNK
        in_band = (qc >= j) & (qc <= j + N_LEFT_CHUNKS)
        off = rows - 1 - r
        o_ref[:, r, :] = jnp.where(in_band, r_ref[:, off:off + width] * LOG2E, NEG_INF)


def _group_bias(rel_table, group_chunks):
    heads = rel_table.shape[0]
    rows = group_chunks * CHUNK
    width = (group_chunks + N_LEFT_CHUNKS) * CHUNK
    dist = (rows - 1 + N_LEFT_CHUNKS * CHUNK) - jnp.arange(rows - 1 + width)
    rev = rel_table[:, jnp.clip(dist, -MAX_REL, MAX_REL) + MAX_REL]
    return pl.pallas_call(
        _bias_kernel,
        out_shape=jax.ShapeDtypeStruct((heads, rows, width), _F32),
        name="rel_bias",
    )(rev)


def _attn_kernel(x_ref, gn_ref, wq_ref, kprev_ref, kcur_ref, vprev_ref, vcur_ref, bias_ref, wo_ref,
                 o_ref, q_scr, a_scr, k_scr, v_scr, s_scr, p_scr,
                 *, heads, head_dim, group_chunks):
    tq = x_ref.shape[1]
    rows = group_chunks * CHUNK
    width = (group_chunks + N_LEFT_CHUNKS) * CHUNK
    pad = N_LEFT_CHUNKS * CHUNK
    pair = V7X_LANES // head_dim
    n_slots = s_scr.shape[0]
    i = pl.program_id(1)

    d = x_ref.shape[2]
    k_scr[:pad, :d] = kprev_ref[0]
    k_scr[pad:, :d] = kcur_ref[0]
    v_scr[:pad, :d] = vprev_ref[0]
    v_scr[pad:, :d] = vcur_ref[0]

    x = x_ref[0]
    h = _rms_norm(x, gn_ref[...]).astype(_BF16)
    q_scr[...] = (_dot(h, wq_ref[...]) * (head_dim ** -0.5 * LOG2E)).astype(_BF16)

    lane = lax.broadcasted_iota(jnp.int32, (1, V7X_LANES), 1)
    kcol = lax.broadcasted_iota(jnp.int32, (1, width), 1)

    def group_body(gi, carry, *, mask_pad):
        r0 = pl.multiple_of(gi * rows, rows)
        valid = (i * tq + r0 + kcol) >= pad
        denom = {}

        def lanes_of(slab):
            return slice(slab * V7X_LANES, (slab + 1) * V7X_LANES)

        def mine_of(sub):
            return (lane >= sub * head_dim) & (lane < (sub + 1) * head_dim)

        def scores(slab):
            q2 = q_scr[pl.ds(r0, rows), lanes_of(slab)]
            qm = jnp.concatenate(
                [jnp.where(mine_of(sub), q2, jnp.zeros_like(q2)) for sub in range(pair)], axis=0)
            k2 = k_scr[pl.ds(r0, width), lanes_of(slab)]
            s = lax.dot_general(qm, k2, (((1,), (1,)), ((), ())),
                                preferred_element_type=_F32)
            s = s + bias_ref[slab]
            if mask_pad:
                s = jnp.where(valid, s, NEG_INF)
            s_scr[slab % n_slots] = s

        def softmax(slab):
            s = s_scr[slab % n_slots]
            p = jnp.exp2(s - jnp.max(s, axis=-1, keepdims=True))
            denom[slab] = jnp.sum(p, axis=-1, keepdims=True)
            p_scr[slab % n_slots] = p.astype(_BF16)

        def weighted_values(slab):
            v2 = v_scr[pl.ds(r0, width), lanes_of(slab)]
            o = _dot(p_scr[slab % n_slots], v2) * (1.0 / denom.pop(slab))
            out = o[:rows]
            for sub in range(1, pair):
                out = jnp.where(mine_of(sub), o[sub * rows:(sub + 1) * rows], out)
            a_scr[pl.ds(r0, rows), lanes_of(slab)] = out.astype(_BF16)

        n_slabs = heads // pair
        for t in range(n_slabs + PV_LAG):
            if t < n_slabs:
                scores(t)
            if SOFTMAX_LAG <= t < n_slabs + SOFTMAX_LAG:
                softmax(t - SOFTMAX_LAG)
            if t >= PV_LAG:
                weighted_values(t - PV_LAG)
        return carry

    n_groups = tq // rows

    @pl.when(i * tq < pad)
    def _():
        lax.fori_loop(0, n_groups, functools.partial(group_body, mask_pad=True), 0)

    @pl.when(i * tq >= pad)
    def _():
        lax.fori_loop(0, n_groups, functools.partial(group_body, mask_pad=False), 0)

    o_ref[0] = x + _dot(a_scr[...], wo_ref[...])


def _attn_layer(x3, layer, g_norm, w_q, bias, w_o, k3, v3, heads):
    b, s, d = x3.shape
    head_dim = d // heads
    tq = ATTN_TILE
    rows = Q_GROUP_CHUNKS * CHUNK
    width = (Q_GROUP_CHUNKS + N_LEFT_CHUNKS) * CHUNK
    pad = N_LEFT_CHUNKS * CHUNK
    pair = V7X_LANES // head_dim
    assert tq % pad == 0
    bias = bias.reshape(heads // pair, pair * rows, width)
    kern = functools.partial(_attn_kernel, heads=heads, head_dim=head_dim,
                             group_chunks=Q_GROUP_CHUNKS)
    lo_spec = pl.BlockSpec((1, pad, d), lambda bi, i: (bi, jnp.maximum(i * (tq // pad) - 1, 0), 0))
    hi_spec = pl.BlockSpec((1, tq, d), lambda bi, i: (bi, i, 0))
    return pl.pallas_call(
        kern,
        grid=(b, s // tq),
        in_specs=[
            pl.BlockSpec((1, tq, d), lambda bi, i: (bi, i, 0)),
            _layer_spec(g_norm.shape, layer),
            _layer_spec(w_q.shape, layer),
            lo_spec, hi_spec, lo_spec, hi_spec,
            _const_spec(bias.shape),
            _layer_spec(w_o.shape, layer),
        ],
        out_specs=pl.BlockSpec((1, tq, d), lambda bi, i: (bi, i, 0)),
        out_shape=jax.ShapeDtypeStruct((b, s, d), _F32),
        scratch_shapes=[pltpu.VMEM((tq, d), _BF16), pltpu.VMEM((tq, d), _BF16),
                        pltpu.VMEM((pad + tq, d + PITCH_PAD), _BF16),
                        pltpu.VMEM((pad + tq, d + PITCH_PAD), _BF16),
                        pltpu.VMEM((ATTN_SLOTS, pair * rows, width), _F32),
                        pltpu.VMEM((ATTN_SLOTS, pair * rows, width), _BF16)],
        compiler_params=pltpu.CompilerParams(
            dimension_semantics=("arbitrary", "arbitrary"),
            vmem_limit_bytes=V7X_VMEM_LIMIT_BYTES),
        name="banded_attention",
    )(x3, g_norm, w_q, k3, k3, v3, v3, bias, w_o)


def kernel(x, a_norm, a_w_in, a_sgu_norm, a_w_spatial, a_b_spatial, a_w_out, kv_norm, w_kv,
           b_norm, b_w_q, b_rel_bias, b_w_o, ffn_norm, ffn_w_gate_up, ffn_w_down, final_norm):
    b, s, d = x.shape
    n_a = a_norm.shape[0]
    depth = ffn_norm.shape[0]
    heads = b_rel_bias.shape[1]
    assert s % ATTN_TILE == 0 and (b * s) % max(GMLP_TILE, FFN_TILE, KV_TILE) == 0
    assert ATTN_TILE % (Q_GROUP_CHUNKS * CHUNK) == 0 and GMLP_TILE % A_CHUNK == 0

    a_norm3 = a_norm[:, None, :]
    a_sgu3 = a_sgu_norm[:, None, :]
    a_bs_t = jnp.swapaxes(a_b_spatial, 1, 2)
    b_norm3 = b_norm[:, None, :]
    b_w_q16, b_w_o16 = b_w_q.astype(_BF16), b_w_o.astype(_BF16)
    ffn_norm3 = ffn_norm[:, None, :]
    g_final = final_norm.reshape(1, d)

    x2 = x.reshape(b * s, d)
    k3 = v3 = None
    for layer in range(depth):
        if layer < n_a:
            x2 = _gmlp_layer(x2, layer, a_norm3, a_w_in, a_sgu3, a_w_spatial, a_bs_t, a_w_out)
        else:
            x3 = x2.reshape(b, s, d)
            if layer == n_a:
                k2, v2 = _shared_kv(x2, kv_norm, w_kv)
                k3, v3 = k2.reshape(b, s, d), v2.reshape(b, s, d)
            li = layer - n_a
            bias = _group_bias(b_rel_bias[li], Q_GROUP_CHUNKS)
            x2 = _attn_layer(x3, li, b_norm3, b_w_q16, bias, b_w_o16, k3, v3,
                             heads).reshape(b * s, d)
        x2 = _ffn_layer(x2, layer, ffn_norm3, ffn_w_gate_up, ffn_w_down, g_final,
                        final_norm=(layer == depth - 1))
    return x2.reshape(b, s, d)
```

```python
import functools
import math

import jax
import jax.numpy as jnp
from jax import lax
from jax.experimental import pallas as pl
from jax.experimental.pallas import tpu as pltpu

EPS = 1e-6
CHUNK = 64
A_CHUNK = 128
N_LEFT_CHUNKS = 8
MAX_REL = 256
NEG_INF = -1e30
LOG2E = math.log2(math.e)

V7X_LANES = 128
V7X_MXU_DIM = 256
V7X_VMEM_LIMIT_BYTES = 56 * 1024 * 1024

GMLP_TILE = 512
FFN_TILE = 512
KV_TILE = 1024
SOFTMAX_LAG = 1
PV_LAG = 2
Q_GROUP_CHUNKS = 2
ATTN_TILE = 512
ATTN_SLOTS = 4
STAGE_DEPTH = 3
PITCH_PAD = V7X_LANES

_BF16 = jnp.bfloat16
_F32 = jnp.float32


def _const_spec(shape):
    nd = len(shape)
    return pl.BlockSpec(shape, lambda *_: (0,) * nd, pipeline_mode=pl.Buffered(1))


def _layer_spec(stacked_shape, layer):
    nd = len(stacked_shape)
    return pl.BlockSpec((None,) + tuple(stacked_shape[1:]),
                        lambda *_: (layer,) + (0,) * (nd - 1), pipeline_mode=pl.Buffered(1))


def _rms_norm(x, g):
    return (x * lax.rsqrt(jnp.mean(x * x, axis=-1, keepdims=True) + EPS)) * g


def _dot(a, b):
    return jnp.dot(a, b, preferred_element_type=_F32)


def _gelu(x):
    return 0.5 * x * (1.0 + lax.erf(x * (1.0 / math.sqrt(2.0))))


class _WeightStream:
    def __init__(self, sources, stage, sems):
        self.sources, self.stage, self.sems = sources, stage, sems
        self.depth = stage.shape[0]

    def _copy(self, k):
        slot = k % self.depth
        return pltpu.make_async_copy(self.sources[k], self.stage.at[slot], self.sems.at[slot])

    def prime(self):
        for k in range(min(self.depth, len(self.sources))):
            self._copy(k).start()

    def take(self, k, dst):
        self._copy(k).wait()
        dst[...] = self.stage[k % self.depth].astype(_BF16)
        if k + self.depth < len(self.sources):
            self._copy(k + self.depth).start()


def _gmlp_kernel(x_ref, gn_ref, win_hbm, gsgu_ref, ws_ref, bs_ref, wout_hbm, o_ref,
                 win16, wout16, v_scr, t_scr, in_stage, out_stage, in_sems, out_sems,
                 *, layer, n_groups, gdim):
    tm, d = x_ref.shape
    dff = n_groups * gdim

    def gate_cols(g):
        return slice(dff + g * gdim, dff + (g + 1) * gdim)

    def lin_cols(g):
        return slice(g * gdim, (g + 1) * gdim)

    def body(load_weights):
        if load_weights:
            in_chunks = [gate_cols(g) for g in range(n_groups)] + [lin_cols(g) for g in range(n_groups)]
            w_in = _WeightStream([win_hbm.at[layer, :, c] for c in in_chunks], in_stage, in_sems)
            w_out = _WeightStream([wout_hbm.at[layer, lin_cols(g), :] for g in range(n_groups)],
                                  out_stage, out_sems)
            w_in.prime()
            w_out.prime()

        x = x_ref[...]
        h = _rms_norm(x, gn_ref[...]).astype(_BF16)

        ss = jnp.zeros((tm, 1), _F32)
        for g in range(n_groups):
            if load_weights:
                w_in.take(g, win16.at[:, gate_cols(g)])
            z = _gelu(_dot(h, win16[:, gate_cols(g)]))
            v_scr[:, lin_cols(g)] = z
            ss = ss + jnp.sum(z * z, axis=-1, keepdims=True)
        r = lax.rsqrt(ss * (1.0 / dff) + EPS)

        row = lax.broadcasted_iota(jnp.int32, (A_CHUNK, A_CHUNK), 0) // CHUNK
        col = lax.broadcasted_iota(jnp.int32, (A_CHUNK, A_CHUNK), 1) // CHUNK
        causal = row >= col

        for g in range(n_groups):
            cols = lin_cols(g)
            if load_weights:
                w_in.take(n_groups + g, win16.at[:, cols])
                w_out.take(g, wout16.at[cols, :d])
            vn = ((v_scr[:, cols] * r) * gsgu_ref[:, cols]).astype(_BF16)
            w = jnp.where(causal, ws_ref[g], 0.0).astype(_BF16)
            bias = bs_ref[:, g:g + 1]
            sp = jnp.concatenate(
                [_dot(w, vn[k * A_CHUNK:(k + 1) * A_CHUNK]) + bias for k in range(tm // A_CHUNK)],
                axis=0)
            u = _gelu(_dot(h, win16[:, cols]))
            t_scr[:, cols] = (u * sp).astype(_BF16)

        o_ref[...] = x + _dot(t_scr[...], wout16[:, :d])

    first = pl.program_id(0) == 0
    pl.when(first)(functools.partial(body, True))
    pl.when(jnp.logical_not(first))(functools.partial(body, False))


def _gmlp_layer(x2, layer, g_norm, w_in, g_sgu, w_s, b_s_t, w_out):
    t, d = x2.shape
    n_groups = w_s.shape[1]
    dff = w_out.shape[1]
    gdim = dff // n_groups
    tm = GMLP_TILE
    kern = functools.partial(_gmlp_kernel, layer=layer, n_groups=n_groups, gdim=gdim)
    return pl.pallas_call(
        kern,
        grid=(t // tm,),
        in_specs=[
            pl.BlockSpec((tm, d), lambda i: (i, 0)),
            _layer_spec(g_norm.shape, layer),
            pl.BlockSpec(memory_space=pl.ANY),
            _layer_spec(g_sgu.shape, layer),
            _layer_spec(w_s.shape, layer),
            _layer_spec(b_s_t.shape, layer),
            pl.BlockSpec(memory_space=pl.ANY),
        ],
        out_specs=pl.BlockSpec((tm, d), lambda i: (i, 0)),
        out_shape=jax.ShapeDtypeStruct((t, d), _F32),
        scratch_shapes=[pltpu.VMEM((d, 2 * dff + PITCH_PAD), _BF16),
                        pltpu.VMEM((dff, d + PITCH_PAD), _BF16),
                        pltpu.VMEM((tm, dff), _F32), pltpu.VMEM((tm, dff), _BF16),
                        pltpu.VMEM((2 * STAGE_DEPTH, d, gdim), _F32),
                        pltpu.VMEM((STAGE_DEPTH, gdim, d), _F32),
                        pltpu.SemaphoreType.DMA((2 * STAGE_DEPTH,)),
                        pltpu.SemaphoreType.DMA((STAGE_DEPTH,))],
        compiler_params=pltpu.CompilerParams(
            dimension_semantics=("arbitrary",), vmem_limit_bytes=V7X_VMEM_LIMIT_BYTES),
        name="gmlp_mixer",
    )(x2, g_norm, w_in, g_sgu, w_s, b_s_t, w_out)


def _ffn_kernel(x_ref, gn_ref, wgu_hbm, wd_hbm, fn_ref, o_ref,
                wgu16, wd16, a_scr, gu_stage, d_stage, gu_sems, d_sems,
                *, layer, hidden, final_norm):
    ck = V7X_MXU_DIM
    n_chunks = hidden // ck

    def gate_cols(c):
        return slice(c * ck, (c + 1) * ck)

    def up_cols(c):
        return slice(hidden + c * ck, hidden + (c + 1) * ck)

    def body(load_weights):
        if load_weights:
            gu_chunks = [cols(c) for c in range(n_chunks) for cols in (gate_cols, up_cols)]
            w_gu = _WeightStream([wgu_hbm.at[layer, :, c] for c in gu_chunks], gu_stage, gu_sems)
            w_d = _WeightStream([wd_hbm.at[layer, gate_cols(c), :] for c in range(n_chunks)],
                                d_stage, d_sems)
            w_gu.prime()
            w_d.prime()

        x = x_ref[...]
        h = _rms_norm(x, gn_ref[...]).astype(_BF16)
        for c in range(n_chunks):
            if load_weights:
                w_gu.take(2 * c, wgu16.at[:, gate_cols(c)])
                w_gu.take(2 * c + 1, wgu16.at[:, up_cols(c)])
                w_d.take(c, wd16.at[gate_cols(c), :])
            gate = _dot(h, wgu16[:, gate_cols(c)])
            up = _dot(h, wgu16[:, up_cols(c)])
            a_scr[:, gate_cols(c)] = ((gate * jax.nn.sigmoid(gate)) * up).astype(_BF16)
        y = x + _dot(a_scr[...], wd16[...])
        if final_norm:
            y = _rms_norm(y, fn_ref[...])
        o_ref[...] = y

    first = pl.program_id(0) == 0
    pl.when(first)(functools.partial(body, True))
    pl.when(jnp.logical_not(first))(functools.partial(body, False))


def _ffn_layer(x2, layer, g_norm, w_gate_up, w_down, g_final, final_norm):
    t, d = x2.shape
    hidden = w_down.shape[1]
    ck = V7X_MXU_DIM
    assert hidden % ck == 0
    tm = FFN_TILE
    kern = functools.partial(_ffn_kernel, layer=layer, hidden=hidden, final_norm=final_norm)
    return pl.pallas_call(
        kern,
        grid=(t // tm,),
        in_specs=[
            pl.BlockSpec((tm, d), lambda i: (i, 0)),
            _layer_spec(g_norm.shape, layer),
            pl.BlockSpec(memory_space=pl.ANY),
            pl.BlockSpec(memory_space=pl.ANY),
            _const_spec((1, d)),
        ],
        out_specs=pl.BlockSpec((tm, d), lambda i: (i, 0)),
        out_shape=jax.ShapeDtypeStruct((t, d), _F32),
        scratch_shapes=[pltpu.VMEM((d, 2 * hidden), _BF16), pltpu.VMEM((hidden, d), _BF16),
                        pltpu.VMEM((tm, hidden), _BF16),
                        pltpu.VMEM((2 * STAGE_DEPTH, d, ck), _F32),
                        pltpu.VMEM((STAGE_DEPTH, ck, d), _F32),
                        pltpu.SemaphoreType.DMA((2 * STAGE_DEPTH,)),
                        pltpu.SemaphoreType.DMA((STAGE_DEPTH,))],
        compiler_params=pltpu.CompilerParams(
            dimension_semantics=("arbitrary",), vmem_limit_bytes=V7X_VMEM_LIMIT_BYTES),
        name="swiglu_ffn",
    )(x2, g_norm, w_gate_up, w_down, g_final)


def _kv_kernel(x_ref, gn_ref, w_ref, k_ref, v_ref):
    d = x_ref.shape[-1]
    h = _rms_norm(x_ref[...], gn_ref[...]).astype(_BF16)
    kv = _dot(h, w_ref[...])
    k_ref[...] = kv[:, :d].astype(_BF16)
    v_ref[...] = kv[:, d:].astype(_BF16)


def _shared_kv(x2, g_norm, w_kv):
    t, d = x2.shape
    tk = KV_TILE
    out = jax.ShapeDtypeStruct((t, d), _BF16)
    return pl.pallas_call(
        _kv_kernel,
        grid=(t // tk,),
        in_specs=[
            pl.BlockSpec((tk, d), lambda i: (i, 0)),
            _const_spec((1, d)),
            _const_spec((d, 2 * d)),
        ],
        out_specs=[pl.BlockSpec((tk, d), lambda i: (i, 0))] * 2,
        out_shape=[out, out],
        compiler_params=pltpu.CompilerParams(
            dimension_semantics=("arbitrary",), vmem_limit_bytes=V7X_VMEM_LIMIT_BYTES),
        name="shared_kv",
    )(x2, g_norm.reshape(1, d), w_kv.astype(_BF16))


def _bias_kernel(r_ref, o_ref):
    _, rows, width = o_ref.shape
    qc = lax.broadcasted_iota(jnp.int32, (1, width), 1) // CHUNK
    for r in range(rows):
        j = r // CHUNK
        in_band = (qc >= j) & (qc <= j + N_LEFT_CHUNKS)
        off = rows - 1 - r
        o_ref[:, r, :] = jnp.where(in_band, r_ref[:, off:off + width] * LOG2E, NEG_INF)


def _group_bias(rel_table, group_chunks):
    heads = rel_table.shape[0]
    rows = group_chunks * CHUNK
    width = (group_chunks + N_LEFT_CHUNKS) * CHUNK
    dist = (rows - 1 + N_LEFT_CHUNKS * CHUNK) - jnp.arange(rows - 1 + width)
    rev = rel_table[:, jnp.clip(dist, -MAX_REL, MAX_REL) + MAX_REL]
    return pl.pallas_call(
        _bias_kernel,
        out_shape=jax.ShapeDtypeStruct((heads, rows, width), _F32),
        name="rel_bias",
    )(rev)


def _attn_kernel(x_ref, gn_ref, wq_ref, kprev_ref, kcur_ref, vprev_ref, vcur_ref, bias_ref, wo_ref,
                 o_ref, q_scr, a_scr, k_scr, v_scr, s_scr,
                 *, heads, head_dim, group_chunks):
    tq = x_ref.shape[1]
    rows = group_chunks * CHUNK
    width = (group_chunks + N_LEFT_CHUNKS) * CHUNK
    pad = N_LEFT_CHUNKS * CHUNK
    pair = V7X_LANES // head_dim
    n_slots = s_scr.shape[0]
    i = pl.program_id(1)

    d = x_ref.shape[2]
    k_scr[:pad, :d] = kprev_ref[0]
    k_scr[pad:, :d] = kcur_ref[0]
    v_scr[:pad, :d] = vprev_ref[0]
    v_scr[pad:, :d] = vcur_ref[0]

    x = x_ref[0]
    h = _rms_norm(x, gn_ref[...]).astype(_BF16)
    q_scr[...] = (_dot(h, wq_ref[...]) * (head_dim ** -0.5 * LOG2E)).astype(_BF16)

    lane = lax.broadcasted_iota(jnp.int32, (1, V7X_LANES), 1)
    kcol = lax.broadcasted_iota(jnp.int32, (1, width), 1)

    def group_body(gi, carry, *, mask_pad):
        r0 = pl.multiple_of(gi * rows, rows)
        valid = (i * tq + r0 + kcol) >= pad
        denom = {}
        probs = {}

        def lanes_of(slab):
            return slice(slab * V7X_LANES, (slab + 1) * V7X_LANES)

        def mine_of(sub):
            return (lane >= sub * head_dim) & (lane < (sub + 1) * head_dim)

        def scores(slab):
            q2 = q_scr[pl.ds(r0, rows), lanes_of(slab)]
            qm = jnp.concatenate(
                [jnp.where(mine_of(sub), q2, jnp.zeros_like(q2)) for sub in range(pair)], axis=0)
            k2 = k_scr[pl.ds(r0, width), lanes_of(slab)]
            s = lax.dot_general(qm, k2, (((1,), (1,)), ((), ())),
                                preferred_element_type=_F32)
            s = s + bias_ref[slab]
            if mask_pad:
                s = jnp.where(valid, s, NEG_INF)
            s_scr[slab % n_slots] = s

        def softmax(slab):
            s = s_scr[slab % n_slots]
            p = jnp.exp2(s - jnp.max(s, axis=-1, keepdims=True))
            denom[slab] = jnp.sum(p, axis=-1, keepdims=True)
            probs[slab] = p.astype(_BF16)

        def weighted_values(slab):
            v2 = v_scr[pl.ds(r0, width), lanes_of(slab)]
            o = _dot(probs.pop(slab), v2) * (1.0 / denom.pop(slab))
            out = o[:rows]
            for sub in range(1, pair):
                out = jnp.where(mine_of(sub), o[sub * rows:(sub + 1) * rows], out)
            a_scr[pl.ds(r0, rows), lanes_of(slab)] = out.astype(_BF16)

        n_slabs = heads // pair
        for t in range(n_slabs + PV_LAG):
            if t < n_slabs:
                scores(t)
            if SOFTMAX_LAG <= t < n_slabs + SOFTMAX_LAG:
                softmax(t - SOFTMAX_LAG)
            if t >= PV_LAG:
                weighted_values(t - PV_LAG)
        return carry

    n_groups = tq // rows

    @pl.when(i * tq < pad)
    def _():
        lax.fori_loop(0, n_groups, functools.partial(group_body, mask_pad=True), 0)

    @pl.when(i * tq >= pad)
    def _():
        lax.fori_loop(0, n_groups, functools.partial(group_body, mask_pad=False), 0)

    o_ref[0] = x + _dot(a_scr[...], wo_ref[...])


def _attn_layer(x3, layer, g_norm, w_q, bias, w_o, k3, v3, heads):
    b, s, d = x3.shape
    head_dim = d // heads
    tq = ATTN_TILE
    rows = Q_GROUP_CHUNKS * CHUNK
    width = (Q_GROUP_CHUNKS + N_LEFT_CHUNKS) * CHUNK
    pad = N_LEFT_CHUNKS * CHUNK
    pair = V7X_LANES // head_dim
    assert tq % pad == 0
    bias = bias.reshape(heads // pair, pair * rows, width)
    kern = functools.partial(_attn_kernel, heads=heads, head_dim=head_dim,
                             group_chunks=Q_GROUP_CHUNKS)
    lo_spec = pl.BlockSpec((1, pad, d), lambda bi, i: (bi, jnp.maximum(i * (tq // pad) - 1, 0), 0))
    hi_spec = pl.BlockSpec((1, tq, d), lambda bi, i: (bi, i, 0))
    return pl.pallas_call(
        kern,
        grid=(b, s // tq),
        in_specs=[
            pl.BlockSpec((1, tq, d), lambda bi, i: (bi, i, 0)),
            _layer_spec(g_norm.shape, layer),
            _layer_spec(w_q.shape, layer),
            lo_spec, hi_spec, lo_spec, hi_spec,
            _const_spec(bias.shape),
            _layer_spec(w_o.shape, layer),
        ],
        out_specs=pl.BlockSpec((1, tq, d), lambda bi, i: (bi, i, 0)),
        out_shape=jax.ShapeDtypeStruct((b, s, d), _F32),
        scratch_shapes=[pltpu.VMEM((tq, d), _BF16), pltpu.VMEM((tq, d), _BF16),
                        pltpu.VMEM((pad + tq, d + PITCH_PAD), _BF16),
                        pltpu.VMEM((pad + tq, d + PITCH_PAD), _BF16),
                        pltpu.VMEM((ATTN_SLOTS, pair * rows, width), _F32)],
        compiler_params=pltpu.CompilerParams(
            dimension_semantics=("arbitrary", "arbitrary"),
            vmem_limit_bytes=V7X_VMEM_LIMIT_BYTES),
        name="banded_attention",
    )(x3, g_norm, w_q, k3, k3, v3, v3, bias, w_o)


def kernel(x, a_norm, a_w_in, a_sgu_norm, a_w_spatial, a_b_spatial, a_w_out, kv_norm, w_kv,
           b_norm, b_w_q, b_rel_bias, b_w_o, ffn_norm, ffn_w_gate_up, ffn_w_down, final_norm):
    b, s, d = x.shape
    n_a = a_norm.shape[0]
    depth = ffn_norm.shape[0]
    heads = b_rel_bias.shape[1]
    assert s % ATTN_TILE == 0 and (b * s) % max(GMLP_TILE, FFN_TILE, KV_TILE) == 0
    assert ATTN_TILE % (Q_GROUP_CHUNKS * CHUNK) == 0 and GMLP_TILE % A_CHUNK == 0

    a_norm3 = a_norm[:, None, :]
    a_sgu3 = a_sgu_norm[:, None, :]
    a_bs_t = jnp.swapaxes(a_b_spatial, 1, 2)
    b_norm3 = b_norm[:, None, :]
    b_w_q16, b_w_o16 = b_w_q.astype(_BF16), b_w_o.astype(_BF16)
    ffn_norm3 = ffn_norm[:, None, :]
    g_final = final_norm.reshape(1, d)

    x2 = x.reshape(b * s, d)
    k3 = v3 = None
    for layer in range(depth):
        if layer < n_a:
            x2 = _gmlp_layer(x2, layer, a_norm3, a_w_in, a_sgu3, a_w_spatial, a_bs_t, a_w_out)
        else:
            x3 = x2.reshape(b, s, d)
            if layer == n_a:
                k2, v2 = _shared_kv(x2, kv_norm, w_kv)
                k3, v3 = k2.reshape(b, s, d), v2.reshape(b, s, d)
            li = layer - n_a
            bias = _group_bias(b_rel_bias[li], Q_GROUP_CHUNKS)
            x2 = _attn_layer(x3, li, b_norm3, b_w_q16, bias, b_w_o16, k3, v3,
                             heads).reshape(b * s, d)
        x2 = _ffn_layer(x2, layer, ffn_norm3, ffn_w_gate_up, ffn_w_down, g_final,
                        final_norm=(layer == depth - 1))
    return x2.reshape(b, s, d)
```

```python
import functools
import math

import jax
import jax.numpy as jnp
from jax import lax
from jax.experimental import pallas as pl
from jax.experimental.pallas import tpu as pltpu

EPS = 1e-6
CHUNK = 64
A_CHUNK = 128
N_LEFT_CHUNKS = 8
MAX_REL = 256
NEG_INF = -1e30
LOG2E = math.log2(math.e)

V7X_LANES = 128
V7X_MXU_DIM = 256
V7X_VMEM_LIMIT_BYTES = 56 * 1024 * 1024

GMLP_TILE = 512
FFN_TILE = 512
KV_TILE = 1024
SOFTMAX_LAG = 1
PV_LAG = 2
Q_GROUP_CHUNKS = 2
ATTN_TILE = 512
ATTN_SLOTS = 4
STAGE_DEPTH = 3
PITCH_PAD = V7X_LANES

_BF16 = jnp.bfloat16
_F32 = jnp.float32


def _const_spec(shape):
    nd = len(shape)
    return pl.BlockSpec(shape, lambda *_: (0,) * nd, pipeline_mode=pl.Buffered(1))


def _layer_spec(stacked_shape, layer):
    nd = len(stacked_shape)
    return pl.BlockSpec((None,) + tuple(stacked_shape[1:]),
                        lambda *_: (layer,) + (0,) * (nd - 1), pipeline_mode=pl.Buffered(1))


def _rms_norm(x, g):
    return (x * lax.rsqrt(jnp.mean(x * x, axis=-1, keepdims=True) + EPS)) * g


def _dot(a, b):
    return jnp.dot(a, b, preferred_element_type=_F32)


def _gelu(x):
    return 0.5 * x * (1.0 + lax.erf(x * (1.0 / math.sqrt(2.0))))


class _WeightStream:
    def __init__(self, sources, stage, sems):
        self.sources, self.stage, self.sems = sources, stage, sems
        self.depth = stage.shape[0]

    def _copy(self, k):
        slot = k % self.depth
        return pltpu.make_async_copy(self.sources[k], self.stage.at[slot], self.sems.at[slot])

    def prime(self):
        for k in range(min(self.depth, len(self.sources))):
            self._copy(k).start()

    def take(self, k, dst):
        self._copy(k).wait()
        dst[...] = self.stage[k % self.depth].astype(_BF16)
        if k + self.depth < len(self.sources):
            self._copy(k + self.depth).start()


def _gmlp_kernel(x_ref, gn_ref, win_hbm, gsgu_ref, ws_ref, bs_ref, wout_hbm, o_ref,
                 win16, wout16, v_scr, t_scr, in_stage, out_stage, in_sems, out_sems,
                 *, layer, n_groups, gdim):
    tm, d = x_ref.shape
    dff = n_groups * gdim

    def gate_cols(g):
        return slice(dff + g * gdim, dff + (g + 1) * gdim)

    def lin_cols(g):
        return slice(g * gdim, (g + 1) * gdim)

    def body(load_weights):
        if load_weights:
            in_chunks = [gate_cols(g) for g in range(n_groups)] + [lin_cols(g) for g in range(n_groups)]
            w_in = _WeightStream([win_hbm.at[layer, :, c] for c in in_chunks], in_stage, in_sems)
            w_out = _WeightStream([wout_hbm.at[layer, lin_cols(g), :] for g in range(n_groups)],
                                  out_stage, out_sems)
            w_in.prime()
            w_out.prime()

        x = x_ref[...]
        h = _rms_norm(x, gn_ref[...]).astype(_BF16)

        ss = jnp.zeros((tm, 1), _F32)
        for g in range(n_groups):
            if load_weights:
                w_in.take(g, win16.at[:, gate_cols(g)])
            z = _gelu(_dot(h, win16[:, gate_cols(g)]))
            v_scr[:, lin_cols(g)] = z
            ss = ss + jnp.sum(z * z, axis=-1, keepdims=True)
        r = lax.rsqrt(ss * (1.0 / dff) + EPS)

        row = lax.broadcasted_iota(jnp.int32, (A_CHUNK, A_CHUNK), 0) // CHUNK
        col = lax.broadcasted_iota(jnp.int32, (A_CHUNK, A_CHUNK), 1) // CHUNK
        causal = row >= col

        for g in range(n_groups):
            cols = lin_cols(g)
            if load_weights:
                w_in.take(n_groups + g, win16.at[:, cols])
                w_out.take(g, wout16.at[cols, :d])
            vn = ((v_scr[:, cols] * r) * gsgu_ref[:, cols]).astype(_BF16)
            w = jnp.where(causal, ws_ref[g], 0.0).astype(_BF16)
            bias = bs_ref[:, g:g + 1]
            sp = jnp.concatenate(
                [_dot(w, vn[k * A_CHUNK:(k + 1) * A_CHUNK]) + bias for k in range(tm // A_CHUNK)],
                axis=0)
            u = _gelu(_dot(h, win16[:, cols]))
            t_scr[:, cols] = (u * sp).astype(_BF16)

        o_ref[...] = x + _dot(t_scr[...], wout16[:, :d])

    first = pl.program_id(0) == 0
    pl.when(first)(functools.partial(body, True))
    pl.when(jnp.logical_not(first))(functools.partial(body, False))


def _gmlp_layer(x2, layer, g_norm, w_in, g_sgu, w_s, b_s_t, w_out):
    t, d = x2.shape
    n_groups = w_s.shape[1]
    dff = w_out.shape[1]
    gdim = dff // n_groups
    tm = GMLP_TILE
    kern = functools.partial(_gmlp_kernel, layer=layer, n_groups=n_groups, gdim=gdim)
    return pl.pallas_call(
        kern,
        grid=(t // tm,),
        in_specs=[
            pl.BlockSpec((tm, d), lambda i: (i, 0)),
            _layer_spec(g_norm.shape, layer),
            pl.BlockSpec(memory_space=pl.ANY),
            _layer_spec(g_sgu.shape, layer),
            _layer_spec(w_s.shape, layer),
            _layer_spec(b_s_t.shape, layer),
            pl.BlockSpec(memory_space=pl.ANY),
        ],
        out_specs=pl.BlockSpec((tm, d), lambda i: (i, 0)),
        out_shape=jax.ShapeDtypeStruct((t, d), _F32),
        scratch_shapes=[pltpu.VMEM((d, 2 * dff + PITCH_PAD), _BF16),
                        pltpu.VMEM((dff, d + PITCH_PAD), _BF16),
                        pltpu.VMEM((tm, dff), _F32), pltpu.VMEM((tm, dff), _BF16),
                        pltpu.VMEM((2 * STAGE_DEPTH, d, gdim), _F32),
                        pltpu.VMEM((STAGE_DEPTH, gdim, d), _F32),
                        pltpu.SemaphoreType.DMA((2 * STAGE_DEPTH,)),
                        pltpu.SemaphoreType.DMA((STAGE_DEPTH,))],
        compiler_params=pltpu.CompilerParams(
            dimension_semantics=("arbitrary",), vmem_limit_bytes=V7X_VMEM_LIMIT_BYTES),
        name="gmlp_mixer",
    )(x2, g_norm, w_in, g_sgu, w_s, b_s_t, w_out)


def _ffn_kernel(x_ref, gn_ref, wgu_hbm, wd_hbm, fn_ref, o_ref,
                wgu16, wd16, a_scr, gu_stage, d_stage, gu_sems, d_sems,
                *, layer, hidden, final_norm):
    ck = V7X_MXU_DIM
    n_chunks = hidden // ck

    def gate_cols(c):
        return slice(c * ck, (c + 1) * ck)

    def up_cols(c):
        return slice(hidden + c * ck, hidden + (c + 1) * ck)

    def body(load_weights):
        if load_weights:
            gu_chunks = [cols(c) for c in range(n_chunks) for cols in (gate_cols, up_cols)]
            w_gu = _WeightStream([wgu_hbm.at[layer, :, c] for c in gu_chunks], gu_stage, gu_sems)
            w_d = _WeightStream([wd_hbm.at[layer, gate_cols(c), :] for c in range(n_chunks)],
                                d_stage, d_sems)
            w_gu.prime()
            w_d.prime()

        x = x_ref[...]
        h = _rms_norm(x, gn_ref[...]).astype(_BF16)
        for c in range(n_chunks):
            if load_weights:
                w_gu.take(2 * c, wgu16.at[:, gate_cols(c)])
                w_gu.take(2 * c + 1, wgu16.at[:, up_cols(c)])
                w_d.take(c, wd16.at[gate_cols(c), :])
            gate = _dot(h, wgu16[:, gate_cols(c)])
            up = _dot(h, wgu16[:, up_cols(c)])
            a_scr[:, gate_cols(c)] = ((gate * jax.nn.sigmoid(gate)) * up).astype(_BF16)
        y = x + _dot(a_scr[...], wd16[...])
        if final_norm:
            y = _rms_norm(y, fn_ref[...])
        o_ref[...] = y

    first = pl.program_id(0) == 0
    pl.when(first)(functools.partial(body, True))
    pl.when(jnp.logical_not(first))(functools.partial(body, False))


def _ffn_layer(x2, layer, g_norm, w_gate_up, w_down, g_final, final_norm):
    t, d = x2.shape
    hidden = w_down.shape[1]
    ck = V7X_MXU_DIM
    assert hidden % ck == 0
    tm = FFN_TILE
    kern = functools.partial(_ffn_kernel, layer=layer, hidden=hidden, final_norm=final_norm)
    return pl.pallas_call(
        kern,
        grid=(t // tm,),
        in_specs=[
            pl.BlockSpec((tm, d), lambda i: (i, 0)),
            _layer_spec(g_norm.shape, layer),
            pl.BlockSpec(memory_space=pl.ANY),
            pl.BlockSpec(memory_space=pl.ANY),
            _const_spec((1, d)),
        ],
        out_specs=pl.BlockSpec((tm, d), lambda i: (i, 0)),
        out_shape=jax.ShapeDtypeStruct((t, d), _F32),
        scratch_shapes=[pltpu.VMEM((d, 2 * hidden), _BF16), pltpu.VMEM((hidden, d), _BF16),
                        pltpu.VMEM((tm, hidden), _BF16),
                        pltpu.VMEM((2 * STAGE_DEPTH, d, ck), _F32),
                        pltpu.VMEM((STAGE_DEPTH, ck, d), _F32),
                        pltpu.SemaphoreType.DMA((2 * STAGE_DEPTH,)),
                        pltpu.SemaphoreType.DMA((STAGE_DEPTH,))],
        compiler_params=pltpu.CompilerParams(
            dimension_semantics=("arbitrary",), vmem_limit_bytes=V7X_VMEM_LIMIT_BYTES),
        name="swiglu_ffn",
    )(x2, g_norm, w_gate_up, w_down, g_final)


def _kv_kernel(x_ref, gn_ref, w_ref, k_ref, v_ref):
    d = x_ref.shape[-1]
    h = _rms_norm(x_ref[...], gn_ref[...]).astype(_BF16)
    kv = _dot(h, w_ref[...])
    k_ref[...] = kv[:, :d].astype(_BF16)
    v_ref[...] = kv[:, d:].astype(_BF16)


def _shared_kv(x2, g_norm, w_kv):
    t, d = x2.shape
    tk = KV_TILE
    out = jax.ShapeDtypeStruct((t, d), _BF16)
    return pl.pallas_call(
        _kv_kernel,
        grid=(t // tk,),
        in_specs=[
            pl.BlockSpec((tk, d), lambda i: (i, 0)),
            _const_spec((1, d)),
            _const_spec((d, 2 * d)),
        ],
        out_specs=[pl.BlockSpec((tk, d), lambda i: (i, 0))] * 2,
        out_shape=[out, out],
        compiler_params=pltpu.CompilerParams(
            dimension_semantics=("arbitrary",), vmem_limit_bytes=V7X_VMEM_LIMIT_BYTES),
        name="shared_kv",
    )(x2, g_norm.reshape(1, d), w_kv.astype(_BF16))


def _bias_kernel(r_ref, o_ref):
    _, rows, width = o_ref.shape
    qc = lax.broadcasted_iota(jnp.int32, (1, width), 1) // CHUNK
    for r in range(rows):
        j = r // CHUNK
        in_band = (qc >= j) & (qc <= j + N_LEFT_CHUNKS)
        off = rows - 1 - r
        o_ref[:, r, :] = jnp.where(in_band, r_ref[:, off:off + width] * LOG2E, NEG_INF)


def _group_bias(rel_table, group_chunks):
    heads = rel_table.shape[0]
    rows = group_chunks * CHUNK
    width = (group_chunks + N_LEFT_CHUNKS) * CHUNK
    dist = (rows - 1 + N_LEFT_CHUNKS * CHUNK) - jnp.arange(rows - 1 + width)
    rev = rel_table[:, jnp.clip(dist, -MAX_REL, MAX_REL) + MAX_REL]
    return pl.pallas_call(
        _bias_kernel,
        out_shape=jax.ShapeDtypeStruct((heads, rows, width), _F32),
        name="rel_bias",
    )(rev)


def _attn_kernel(x_ref, gn_ref, wq_ref, kprev_ref, kcur_ref, vprev_ref, vcur_ref, bias_ref, wo_ref,
                 o_ref, q_scr, a_scr, k_scr, v_scr, s_scr,
                 *, heads, head_dim, group_chunks):
    tq = x_ref.shape[1]
    rows = group_chunks * CHUNK
    width = (group_chunks + N_LEFT_CHUNKS) * CHUNK
    pad = N_LEFT_CHUNKS * CHUNK
    pair = V7X_LANES // head_dim
    n_slots = s_scr.shape[0]
    i = pl.program_id(1)

    d = x_ref.shape[2]
    k_scr[:pad, :d] = kprev_ref[0]
    k_scr[pad:, :d] = kcur_ref[0]
    v_scr[:pad, :d] = vprev_ref[0]
    v_scr[pad:, :d] = vcur_ref[0]

    x = x_ref[0]
    h = _rms_norm(x, gn_ref[...]).astype(_BF16)
    q_scr[...] = (_dot(h, wq_ref[...]) * (head_dim ** -0.5 * LOG2E)).astype(_BF16)

    lane = lax.broadcasted_iota(jnp.int32, (1, V7X_LANES), 1)
    kcol = lax.broadcasted_iota(jnp.int32, (1, width), 1)

    n_groups = tq // rows
    n_slabs = heads // pair

    def lanes_of(slab):
        return slice(slab * V7X_LANES, (slab + 1) * V7X_LANES)

    def mine_of(sub):
        return (lane >= sub * head_dim) & (lane < (sub + 1) * head_dim)

    def run_groups(mask_pad):
        denom = {}
        probs = {}

        def scores(k):
            gi, slab = divmod(k, n_slabs)
            r0 = gi * rows
            q2 = q_scr[r0:r0 + rows, lanes_of(slab)]
            qm = jnp.concatenate(
                [jnp.where(mine_of(sub), q2, jnp.zeros_like(q2)) for sub in range(pair)], axis=0)
            k2 = k_scr[r0:r0 + width, lanes_of(slab)]
            s = lax.dot_general(qm, k2, (((1,), (1,)), ((), ())),
                                preferred_element_type=_F32)
            s = s + bias_ref[slab]
            if mask_pad:
                valid = (i * tq + r0 + kcol) >= pad
                s = jnp.where(valid, s, NEG_INF)
            s_scr[k % n_slots] = s

        def softmax(k):
            s = s_scr[k % n_slots]
            p = jnp.exp2(s - jnp.max(s, axis=-1, keepdims=True))
            denom[k] = jnp.sum(p, axis=-1, keepdims=True)
            probs[k] = p.astype(_BF16)

        def weighted_values(k):
            gi, slab = divmod(k, n_slabs)
            r0 = gi * rows
            v2 = v_scr[r0:r0 + width, lanes_of(slab)]
            o = _dot(probs.pop(k), v2) * (1.0 / denom.pop(k))
            out = o[:rows]
            for sub in range(1, pair):
                out = jnp.where(mine_of(sub), o[sub * rows:(sub + 1) * rows], out)
            a_scr[r0:r0 + rows, lanes_of(slab)] = out.astype(_BF16)

        n_items = n_groups * n_slabs
        for t in range(n_items + PV_LAG):
            if t < n_items:
                scores(t)
            if SOFTMAX_LAG <= t < n_items + SOFTMAX_LAG:
                softmax(t - SOFTMAX_LAG)
            if t >= PV_LAG:
                weighted_values(t - PV_LAG)

    pl.when(i * tq < pad)(functools.partial(run_groups, True))
    pl.when(i * tq >= pad)(functools.partial(run_groups, False))

    o_ref[0] = x + _dot(a_scr[...], wo_ref[...])


def _attn_layer(x3, layer, g_norm, w_q, bias, w_o, k3, v3, heads):
    b, s, d = x3.shape
    head_dim = d // heads
    tq = ATTN_TILE
    rows = Q_GROUP_CHUNKS * CHUNK
    width = (Q_GROUP_CHUNKS + N_LEFT_CHUNKS) * CHUNK
    pad = N_LEFT_CHUNKS * CHUNK
    pair = V7X_LANES // head_dim
    assert tq % pad == 0
    bias = bias.reshape(heads // pair, pair * rows, width)
    kern = functools.partial(_attn_kernel, heads=heads, head_dim=head_dim,
                             group_chunks=Q_GROUP_CHUNKS)
    lo_spec = pl.BlockSpec((1, pad, d), lambda bi, i: (bi, jnp.maximum(i * (tq // pad) - 1, 0), 0))
    hi_spec = pl.BlockSpec((1, tq, d), lambda bi, i: (bi, i, 0))
    return pl.pallas_call(
        kern,
        grid=(b, s // tq),
        in_specs=[
            pl.BlockSpec((1, tq, d), lambda bi, i: (bi, i, 0)),
            _layer_spec(g_norm.shape, layer),
            _layer_spec(w_q.shape, layer),
            lo_spec, hi_spec, lo_spec, hi_spec,
            _const_spec(bias.shape),
            _layer_spec(w_o.shape, layer),
        ],
        out_specs=pl.BlockSpec((1, tq, d), lambda bi, i: (bi, i, 0)),
        out_shape=jax.ShapeDtypeStruct((b, s, d), _F32),
        scratch_shapes=[pltpu.VMEM((tq, d), _BF16), pltpu.VMEM((tq, d), _BF16),
                        pltpu.VMEM((pad + tq, d + PITCH_PAD), _BF16),
                        pltpu.VMEM((pad + tq, d + PITCH_PAD), _BF16),
                        pltpu.VMEM((ATTN_SLOTS, pair * rows, width), _F32)],
        compiler_params=pltpu.CompilerParams(
            dimension_semantics=("arbitrary", "arbitrary"),
            vmem_limit_bytes=V7X_VMEM_LIMIT_BYTES),
        name="banded_attention",
    )(x3, g_norm, w_q, k3, k3, v3, v3, bias, w_o)


def kernel(x, a_norm, a_w_in, a_sgu_norm, a_w_spatial, a_b_spatial, a_w_out, kv_norm, w_kv,
           b_norm, b_w_q, b_rel_bias, b_w_o, ffn_norm, ffn_w_gate_up, ffn_w_down, final_norm):
    b, s, d = x.shape
    n_a = a_norm.shape[0]
    depth = ffn_norm.shape[0]
    heads = b_rel_bias.shape[1]
    assert s % ATTN_TILE == 0 and (b * s) % max(GMLP_TILE, FFN_TILE, KV_TILE) == 0
    assert ATTN_TILE % (Q_GROUP_CHUNKS * CHUNK) == 0 and GMLP_TILE % A_CHUNK == 0

    a_norm3 = a_norm[:, None, :]
    a_sgu3 = a_sgu_norm[:, None, :]
    a_bs_t = jnp.swapaxes(a_b_spatial, 1, 2)
    b_norm3 = b_norm[:, None, :]
    b_w_q16, b_w_o16 = b_w_q.astype(_BF16), b_w_o.astype(_BF16)
    ffn_norm3 = ffn_norm[:, None, :]
    g_final = final_norm.reshape(1, d)

    x2 = x.reshape(b * s, d)
    k3 = v3 = None
    for layer in range(depth):
        if layer < n_a:
            x2 = _gmlp_layer(x2, layer, a_norm3, a_w_in, a_sgu3, a_w_spatial, a_bs_t, a_w_out)
        else:
            x3 = x2.reshape(b, s, d)
            if layer == n_a:
                k2, v2 = _shared_kv(x2, kv_norm, w_kv)
                k3, v3 = k2.reshape(b, s, d), v2.reshape(b, s, d)
            li = layer - n_a
            bias = _group_bias(b_rel_bias[li], Q_GROUP_CHUNKS)
            x2 = _attn_layer(x3, li, b_norm3, b_w_q16, bias, b_w_o16, k3, v3,
                             heads).reshape(b * s, d)
        x2 = _ffn_layer(x2, layer, ffn_norm3, ffn_w_gate_up, ffn_w_down, g_final,
                        final_norm=(layer == depth - 1))
    return x2.reshape(b, s, d)
```

```python
import functools
import math

import jax
import jax.numpy as jnp
from jax import lax
from jax.experimental import pallas as pl
from jax.experimental.pallas import tpu as pltpu

EPS = 1e-6
CHUNK = 64
A_CHUNK = 128
N_LEFT_CHUNKS = 8
MAX_REL = 256
NEG_INF = -1e30
LOG2E = math.log2(math.e)

V7X_LANES = 128
V7X_MXU_DIM = 256
V7X_VMEM_LIMIT_BYTES = 56 * 1024 * 1024

GMLP_TILE = 512
FFN_TILE = 512
KV_TILE = 1024
SOFTMAX_LAG = 1
PV_LAG = 2
Q_GROUP_CHUNKS = 2
ATTN_TILE = 512
ATTN_SLOTS = 4
STAGE_DEPTH = 3
PITCH_PAD = V7X_LANES

_BF16 = jnp.bfloat16
_F32 = jnp.float32


def _const_spec(shape):
    nd = len(shape)
    return pl.BlockSpec(shape, lambda *_: (0,) * nd, pipeline_mode=pl.Buffered(1))


def _layer_spec(stacked_shape, layer):
    nd = len(stacked_shape)
    return pl.BlockSpec((None,) + tuple(stacked_shape[1:]),
                        lambda *_: (layer,) + (0,) * (nd - 1), pipeline_mode=pl.Buffered(1))


def _rms_norm(x, g):
    return (x * lax.rsqrt(jnp.mean(x * x, axis=-1, keepdims=True) + EPS)) * g


def _dot(a, b):
    return jnp.dot(a, b, preferred_element_type=_F32)


def _gelu(x):
    return 0.5 * x * (1.0 + lax.erf(x * (1.0 / math.sqrt(2.0))))


class _WeightStream:
    def __init__(self, sources, stage, sems):
        self.sources, self.stage, self.sems = sources, stage, sems
        self.depth = stage.shape[0]

    def _copy(self, k):
        slot = k % self.depth
        return pltpu.make_async_copy(self.sources[k], self.stage.at[slot], self.sems.at[slot])

    def prime(self):
        for k in range(min(self.depth, len(self.sources))):
            self._copy(k).start()

    def take(self, k, dst):
        self._copy(k).wait()
        dst[...] = self.stage[k % self.depth].astype(_BF16)
        if k + self.depth < len(self.sources):
            self._copy(k + self.depth).start()


def _gmlp_kernel(x_ref, gn_ref, win_hbm, gsgu_ref, ws_ref, bs_ref, wout_hbm, o_ref,
                 win16, wout16, v_scr, t_scr, in_stage, out_stage, in_sems, out_sems,
                 *, layer, n_groups, gdim):
    tm, d = x_ref.shape
    dff = n_groups * gdim

    def gate_cols(g):
        return slice(dff + g * gdim, dff + (g + 1) * gdim)

    def lin_cols(g):
        return slice(g * gdim, (g + 1) * gdim)

    def body(load_weights):
        if load_weights:
            in_chunks = [gate_cols(g) for g in range(n_groups)] + [lin_cols(g) for g in range(n_groups)]
            w_in = _WeightStream([win_hbm.at[layer, :, c] for c in in_chunks], in_stage, in_sems)
            w_out = _WeightStream([wout_hbm.at[layer, lin_cols(g), :] for g in range(n_groups)],
                                  out_stage, out_sems)
            w_in.prime()
            w_out.prime()

        x = x_ref[...]
        h = _rms_norm(x, gn_ref[...]).astype(_BF16)

        ss = jnp.zeros((tm, 1), _F32)
        for g in range(n_groups):
            if load_weights:
                w_in.take(g, win16.at[:, gate_cols(g)])
            z = _gelu(_dot(h, win16[:, gate_cols(g)]))
            v_scr[:, lin_cols(g)] = z
            ss = ss + jnp.sum(z * z, axis=-1, keepdims=True)
        r = lax.rsqrt(ss * (1.0 / dff) + EPS)

        row = lax.broadcasted_iota(jnp.int32, (A_CHUNK, A_CHUNK), 0) // CHUNK
        col = lax.broadcasted_iota(jnp.int32, (A_CHUNK, A_CHUNK), 1) // CHUNK
        causal = row >= col

        for g in range(n_groups):
            cols = lin_cols(g)
            if load_weights:
                w_in.take(n_groups + g, win16.at[:, cols])
                w_out.take(g, wout16.at[cols, :d])
            vn = ((v_scr[:, cols] * r) * gsgu_ref[:, cols]).astype(_BF16)
            w = jnp.where(causal, ws_ref[g], 0.0).astype(_BF16)
            bias = bs_ref[:, g:g + 1]
            sp = jnp.concatenate(
                [_dot(w, vn[k * A_CHUNK:(k + 1) * A_CHUNK]) + bias for k in range(tm // A_CHUNK)],
                axis=0)
            u = _gelu(_dot(h, win16[:, cols]))
            t_scr[:, cols] = (u * sp).astype(_BF16)

        o_ref[...] = x + _dot(t_scr[...], wout16[:, :d])

    first = pl.program_id(0) == 0
    pl.when(first)(functools.partial(body, True))
    pl.when(jnp.logical_not(first))(functools.partial(body, False))


def _gmlp_layer(x2, layer, g_norm, w_in, g_sgu, w_s, b_s_t, w_out):
    t, d = x2.shape
    n_groups = w_s.shape[1]
    dff = w_out.shape[1]
    gdim = dff // n_groups
    tm = GMLP_TILE
    kern = functools.partial(_gmlp_kernel, layer=layer, n_groups=n_groups, gdim=gdim)
    return pl.pallas_call(
        kern,
        grid=(t // tm,),
        in_specs=[
            pl.BlockSpec((tm, d), lambda i: (i, 0)),
            _layer_spec(g_norm.shape, layer),
            pl.BlockSpec(memory_space=pl.ANY),
            _layer_spec(g_sgu.shape, layer),
            _layer_spec(w_s.shape, layer),
            _layer_spec(b_s_t.shape, layer),
            pl.BlockSpec(memory_space=pl.ANY),
        ],
        out_specs=pl.BlockSpec((tm, d), lambda i: (i, 0)),
        out_shape=jax.ShapeDtypeStruct((t, d), _F32),
        scratch_shapes=[pltpu.VMEM((d, 2 * dff + PITCH_PAD), _BF16),
                        pltpu.VMEM((dff, d + PITCH_PAD), _BF16),
                        pltpu.VMEM((tm, dff), _F32), pltpu.VMEM((tm, dff), _BF16),
                        pltpu.VMEM((2 * STAGE_DEPTH, d, gdim), _F32),
                        pltpu.VMEM((STAGE_DEPTH, gdim, d), _F32),
                        pltpu.SemaphoreType.DMA((2 * STAGE_DEPTH,)),
                        pltpu.SemaphoreType.DMA((STAGE_DEPTH,))],
        compiler_params=pltpu.CompilerParams(
            dimension_semantics=("arbitrary",), vmem_limit_bytes=V7X_VMEM_LIMIT_BYTES),
        name="gmlp_mixer",
    )(x2, g_norm, w_in, g_sgu, w_s, b_s_t, w_out)


def _ffn_kernel(x_ref, gn_ref, wgu_hbm, wd_hbm, fn_ref, o_ref,
                wgu16, wd16, a_scr, gu_stage, d_stage, gu_sems, d_sems,
                *, layer, hidden, final_norm):
    ck = V7X_MXU_DIM
    n_chunks = hidden // ck

    def gate_cols(c):
        return slice(c * ck, (c + 1) * ck)

    def up_cols(c):
        return slice(hidden + c * ck, hidden + (c + 1) * ck)

    def body(load_weights):
        if load_weights:
            gu_chunks = [cols(c) for c in range(n_chunks) for cols in (gate_cols, up_cols)]
            w_gu = _WeightStream([wgu_hbm.at[layer, :, c] for c in gu_chunks], gu_stage, gu_sems)
            w_d = _WeightStream([wd_hbm.at[layer, gate_cols(c), :] for c in range(n_chunks)],
                                d_stage, d_sems)
            w_gu.prime()
            w_d.prime()

        x = x_ref[...]
        h = _rms_norm(x, gn_ref[...]).astype(_BF16)
        for c in range(n_chunks):
            if load_weights:
                w_gu.take(2 * c, wgu16.at[:, gate_cols(c)])
                w_gu.take(2 * c + 1, wgu16.at[:, up_cols(c)])
                w_d.take(c, wd16.at[gate_cols(c), :])
            gate = _dot(h, wgu16[:, gate_cols(c)])
            up = _dot(h, wgu16[:, up_cols(c)])
            a_scr[:, gate_cols(c)] = ((gate * jax.nn.sigmoid(gate)) * up).astype(_BF16)
        y = x + _dot(a_scr[...], wd16[...])
        if final_norm:
            y = _rms_norm(y, fn_ref[...])
        o_ref[...] = y

    first = pl.program_id(0) == 0
    pl.when(first)(functools.partial(body, True))
    pl.when(jnp.logical_not(first))(functools.partial(body, False))


def _ffn_layer(x2, layer, g_norm, w_gate_up, w_down, g_final, final_norm):
    t, d = x2.shape
    hidden = w_down.shape[1]
    ck = V7X_MXU_DIM
    assert hidden % ck == 0
    tm = FFN_TILE
    kern = functools.partial(_ffn_kernel, layer=layer, hidden=hidden, final_norm=final_norm)
    return pl.pallas_call(
        kern,
        grid=(t // tm,),
        in_specs=[
            pl.BlockSpec((tm, d), lambda i: (i, 0)),
            _layer_spec(g_norm.shape, layer),
            pl.BlockSpec(memory_space=pl.ANY),
            pl.BlockSpec(memory_space=pl.ANY),
            _const_spec((1, d)),
        ],
        out_specs=pl.BlockSpec((tm, d), lambda i: (i, 0)),
        out_shape=jax.ShapeDtypeStruct((t, d), _F32),
        scratch_shapes=[pltpu.VMEM((d, 2 * hidden), _BF16), pltpu.VMEM((hidden, d), _BF16),
                        pltpu.VMEM((tm, hidden), _BF16),
                        pltpu.VMEM((2 * STAGE_DEPTH, d, ck), _F32),
                        pltpu.VMEM((STAGE_DEPTH, ck, d), _F32),
                        pltpu.SemaphoreType.DMA((2 * STAGE_DEPTH,)),
                        pltpu.SemaphoreType.DMA((STAGE_DEPTH,))],
        compiler_params=pltpu.CompilerParams(
            dimension_semantics=("arbitrary",), vmem_limit_bytes=V7X_VMEM_LIMIT_BYTES),
        name="swiglu_ffn",
    )(x2, g_norm, w_gate_up, w_down, g_final)


def _kv_kernel(x_ref, gn_ref, w_ref, k_ref, v_ref):
    d = x_ref.shape[-1]
    h = _rms_norm(x_ref[...], gn_ref[...]).astype(_BF16)
    kv = _dot(h, w_ref[...])
    k_ref[...] = kv[:, :d].astype(_BF16)
    v_ref[...] = kv[:, d:].astype(_BF16)


def _shared_kv(x2, g_norm, w_kv):
    t, d = x2.shape
    tk = KV_TILE
    out = jax.ShapeDtypeStruct((t, d), _BF16)
    return pl.pallas_call(
        _kv_kernel,
        grid=(t // tk,),
        in_specs=[
            pl.BlockSpec((tk, d), lambda i: (i, 0)),
            _const_spec((1, d)),
            _const_spec((d, 2 * d)),
        ],
        out_specs=[pl.BlockSpec((tk, d), lambda i: (i, 0))] * 2,
        out_shape=[out, out],
        compiler_params=pltpu.CompilerParams(
            dimension_semantics=("arbitrary",), vmem_limit_bytes=V7X_VMEM_LIMIT_BYTES),
        name="shared_kv",
    )(x2, g_norm.reshape(1, d), w_kv.astype(_BF16))


def _bias_kernel(r_ref, o_ref):
    _, rows, width = o_ref.shape
    qc = lax.broadcasted_iota(jnp.int32, (1, width), 1) // CHUNK
    for r in range(rows):
        j = r // CHUNK
        in_band = (qc >= j) & (qc <= j + N_LEFT_CHUNKS)
        off = rows - 1 - r
        o_ref[:, r, :] = jnp.where(in_band, r_ref[:, off:off + width] * LOG2E, NEG_INF)


def _group_bias(rel_table, group_chunks):
    heads = rel_table.shape[0]
    rows = group_chunks * CHUNK
    width = (group_chunks + N_LEFT_CHUNKS) * CHUNK
    dist = (rows - 1 + N_LEFT_CHUNKS * CHUNK) - jnp.arange(rows - 1 + width)
    rev = rel_table[:, jnp.clip(dist, -MAX_REL, MAX_REL) + MAX_REL]
    return pl.pallas_call(
        _bias_kernel,
        out_shape=jax.ShapeDtypeStruct((heads, rows, width), _F32),
        name="rel_bias",
    )(rev)


def _attn_kernel(x_ref, gn_ref, wq_ref, kprev_ref, kcur_ref, vprev_ref, vcur_ref, bias_ref, wo_ref,
                 o_ref, q_scr, a_scr, k_scr, v_scr, s_scr,
                 *, heads, head_dim, group_chunks):
    tq = x_ref.shape[1]
    rows = group_chunks * CHUNK
    width = (group_chunks + N_LEFT_CHUNKS) * CHUNK
    pad = N_LEFT_CHUNKS * CHUNK
    pair = V7X_LANES // head_dim
    n_slots = s_scr.shape[0]
    i = pl.program_id(1)

    d = x_ref.shape[2]
    lane = lax.broadcasted_iota(jnp.int32, (1, V7X_LANES), 1)
    kcol = lax.broadcasted_iota(jnp.int32, (1, width), 1)

    n_groups = tq // rows
    n_slabs = heads // pair

    def lanes_of(slab):
        return slice(slab * V7X_LANES, (slab + 1) * V7X_LANES)

    def mine_of(sub):
        return (lane >= sub * head_dim) & (lane < (sub + 1) * head_dim)

    def run_groups(mask_pad):
        denom = {}
        probs = {}

        def scores(k):
            gi, slab = divmod(k, n_slabs)
            r0 = gi * rows
            q2 = q_scr[r0:r0 + rows, lanes_of(slab)]
            qm = jnp.concatenate(
                [jnp.where(mine_of(sub), q2, jnp.zeros_like(q2)) for sub in range(pair)], axis=0)
            k2 = k_scr[r0:r0 + width, lanes_of(slab)]
            s = lax.dot_general(qm, k2, (((1,), (1,)), ((), ())),
                                preferred_element_type=_F32)
            s = s + bias_ref[slab]
            if mask_pad:
                valid = (i * tq + r0 + kcol) >= pad
                s = jnp.where(valid, s, NEG_INF)
            s_scr[k % n_slots] = s

        def softmax(k):
            s = s_scr[k % n_slots]
            p = jnp.exp2(s - jnp.max(s, axis=-1, keepdims=True))
            denom[k] = jnp.sum(p, axis=-1, keepdims=True)
            probs[k] = p.astype(_BF16)

        def weighted_values(k):
            gi, slab = divmod(k, n_slabs)
            r0 = gi * rows
            v2 = v_scr[r0:r0 + width, lanes_of(slab)]
            o = _dot(probs.pop(k), v2) * (1.0 / denom.pop(k))
            out = o[:rows]
            for sub in range(1, pair):
                out = jnp.where(mine_of(sub), o[sub * rows:(sub + 1) * rows], out)
            a_scr[r0:r0 + rows, lanes_of(slab)] = out.astype(_BF16)

        n_items = n_groups * n_slabs
        for t in range(n_items + PV_LAG):
            if t < n_items:
                scores(t)
            if SOFTMAX_LAG <= t < n_items + SOFTMAX_LAG:
                softmax(t - SOFTMAX_LAG)
            if t >= PV_LAG:
                weighted_values(t - PV_LAG)

    def tile_body(mask_pad):
        k_scr[:pad, :d] = kprev_ref[0]
        k_scr[pad:, :d] = kcur_ref[0]
        v_scr[:pad, :d] = vprev_ref[0]
        v_scr[pad:, :d] = vcur_ref[0]

        x = x_ref[0]
        h = _rms_norm(x, gn_ref[...]).astype(_BF16)
        q_scr[...] = (_dot(h, wq_ref[...]) * (head_dim ** -0.5 * LOG2E)).astype(_BF16)
        run_groups(mask_pad)
        o_ref[0] = x + _dot(a_scr[...], wo_ref[...])

    pl.when(i * tq < pad)(functools.partial(tile_body, True))
    pl.when(i * tq >= pad)(functools.partial(tile_body, False))


def _attn_layer(x3, layer, g_norm, w_q, bias, w_o, k3, v3, heads):
    b, s, d = x3.shape
    head_dim = d // heads
    tq = ATTN_TILE
    rows = Q_GROUP_CHUNKS * CHUNK
    width = (Q_GROUP_CHUNKS + N_LEFT_CHUNKS) * CHUNK
    pad = N_LEFT_CHUNKS * CHUNK
    pair = V7X_LANES // head_dim
    assert tq % pad == 0
    bias = bias.reshape(heads // pair, pair * rows, width)
    kern = functools.partial(_attn_kernel, heads=heads, head_dim=head_dim,
                             group_chunks=Q_GROUP_CHUNKS)
    lo_spec = pl.BlockSpec((1, pad, d), lambda bi, i: (bi, jnp.maximum(i * (tq // pad) - 1, 0), 0))
    hi_spec = pl.BlockSpec((1, tq, d), lambda bi, i: (bi, i, 0))
    return pl.pallas_call(
        kern,
        grid=(b, s // tq),
        in_specs=[
            pl.BlockSpec((1, tq, d), lambda bi, i: (bi, i, 0)),
            _layer_spec(g_norm.shape, layer),
            _layer_spec(w_q.shape, layer),
            lo_spec, hi_spec, lo_spec, hi_spec,
            _const_spec(bias.shape),
            _layer_spec(w_o.shape, layer),
        ],
        out_specs=pl.BlockSpec((1, tq, d), lambda bi, i: (bi, i, 0)),
        out_shape=jax.ShapeDtypeStruct((b, s, d), _F32),
        scratch_shapes=[pltpu.VMEM((tq, d), _BF16), pltpu.VMEM((tq, d), _BF16),
                        pltpu.VMEM((pad + tq, d + PITCH_PAD), _BF16),
                        pltpu.VMEM((pad + tq, d + PITCH_PAD), _BF16),
                        pltpu.VMEM((ATTN_SLOTS, pair * rows, width), _F32)],
        compiler_params=pltpu.CompilerParams(
            dimension_semantics=("arbitrary", "arbitrary"),
            vmem_limit_bytes=V7X_VMEM_LIMIT_BYTES),
        name="banded_attention",
    )(x3, g_norm, w_q, k3, k3, v3, v3, bias, w_o)


def kernel(x, a_norm, a_w_in, a_sgu_norm, a_w_spatial, a_b_spatial, a_w_out, kv_norm, w_kv,
           b_norm, b_w_q, b_rel_bias, b_w_o, ffn_norm, ffn_w_gate_up, ffn_w_down, final_norm):
    b, s, d = x.shape
    n_a = a_norm.shape[0]
    depth = ffn_norm.shape[0]
    heads = b_rel_bias.shape[1]
    assert s % ATTN_TILE == 0 and (b * s) % max(GMLP_TILE, FFN_TILE, KV_TILE) == 0
    assert ATTN_TILE % (Q_GROUP_CHUNKS * CHUNK) == 0 and GMLP_TILE % A_CHUNK == 0

    a_norm3 = a_norm[:, None, :]
    a_sgu3 = a_sgu_norm[:, None, :]
    a_bs_t = jnp.swapaxes(a_b_spatial, 1, 2)
    b_norm3 = b_norm[:, None, :]
    b_w_q16, b_w_o16 = b_w_q.astype(_BF16), b_w_o.astype(_BF16)
    ffn_norm3 = ffn_norm[:, None, :]
    g_final = final_norm.reshape(1, d)

    x2 = x.reshape(b * s, d)
    k3 = v3 = None
    for layer in range(depth):
        if layer < n_a:
            x2 = _gmlp_layer(x2, layer, a_norm3, a_w_in, a_sgu3, a_w_spatial, a_bs_t, a_w_out)
        else:
            x3 = x2.reshape(b, s, d)
            if layer == n_a:
                k2, v2 = _shared_kv(x2, kv_norm, w_kv)
                k3, v3 = k2.reshape(b, s, d), v2.reshape(b, s, d)
            li = layer - n_a
            bias = _group_bias(b_rel_bias[li], Q_GROUP_CHUNKS)
            x2 = _attn_layer(x3, li, b_norm3, b_w_q16, bias, b_w_o16, k3, v3,
                             heads).reshape(b * s, d)
        x2 = _ffn_layer(x2, layer, ffn_norm3, ffn_w_gate_up, ffn_w_down, g_final,
                        final_norm=(layer == depth - 1))
    return x2.reshape(b, s, d)
```

```python
import functools
import math

import jax
import jax.numpy as jnp
from jax import lax
from jax.experimental import pallas as pl
from jax.experimental.pallas import tpu as pltpu

EPS = 1e-6
CHUNK = 64
A_CHUNK = 128
N_LEFT_CHUNKS = 8
MAX_REL = 256
NEG_INF = -1e30
LOG2E = math.log2(math.e)

V7X_LANES = 128
V7X_MXU_DIM = 256
V7X_VMEM_LIMIT_BYTES = 56 * 1024 * 1024

GMLP_TILE = 512
FFN_TILE = 512
KV_TILE = 1024
SOFTMAX_LAG = 1
PV_LAG = 4
Q_GROUP_CHUNKS = 2
ATTN_TILE = 512
ATTN_SLOTS = 4
STAGE_DEPTH = 3
PITCH_PAD = V7X_LANES

_BF16 = jnp.bfloat16
_F32 = jnp.float32


def _const_spec(shape):
    nd = len(shape)
    return pl.BlockSpec(shape, lambda *_: (0,) * nd, pipeline_mode=pl.Buffered(1))


def _layer_spec(stacked_shape, layer):
    nd = len(stacked_shape)
    return pl.BlockSpec((None,) + tuple(stacked_shape[1:]),
                        lambda *_: (layer,) + (0,) * (nd - 1), pipeline_mode=pl.Buffered(1))


def _rms_norm(x, g):
    return (x * lax.rsqrt(jnp.mean(x * x, axis=-1, keepdims=True) + EPS)) * g


def _dot(a, b):
    return jnp.dot(a, b, preferred_element_type=_F32)


def _gelu(x):
    return 0.5 * x * (1.0 + lax.erf(x * (1.0 / math.sqrt(2.0))))


class _WeightStream:
    def __init__(self, sources, stage, sems):
        self.sources, self.stage, self.sems = sources, stage, sems
        self.depth = stage.shape[0]

    def _copy(self, k):
        slot = k % self.depth
        return pltpu.make_async_copy(self.sources[k], self.stage.at[slot], self.sems.at[slot])

    def prime(self):
        for k in range(min(self.depth, len(self.sources))):
            self._copy(k).start()

    def take(self, k, dst):
        self._copy(k).wait()
        dst[...] = self.stage[k % self.depth].astype(_BF16)
        if k + self.depth < len(self.sources):
            self._copy(k + self.depth).start()


def _gmlp_kernel(x_ref, gn_ref, win_hbm, gsgu_ref, ws_ref, bs_ref, wout_hbm, o_ref,
                 win16, wout16, v_scr, t_scr, in_stage, out_stage, in_sems, out_sems,
                 *, layer, n_groups, gdim):
    tm, d = x_ref.shape
    dff = n_groups * gdim

    def gate_cols(g):
        return slice(dff + g * gdim, dff + (g + 1) * gdim)

    def lin_cols(g):
        return slice(g * gdim, (g + 1) * gdim)

    def body(load_weights):
        if load_weights:
            in_chunks = [gate_cols(g) for g in range(n_groups)] + [lin_cols(g) for g in range(n_groups)]
            w_in = _WeightStream([win_hbm.at[layer, :, c] for c in in_chunks], in_stage, in_sems)
            w_out = _WeightStream([wout_hbm.at[layer, lin_cols(g), :] for g in range(n_groups)],
                                  out_stage, out_sems)
            w_in.prime()
            w_out.prime()

        x = x_ref[...]
        h = _rms_norm(x, gn_ref[...]).astype(_BF16)

        ss = jnp.zeros((tm, 1), _F32)
        for g in range(n_groups):
            if load_weights:
                w_in.take(g, win16.at[:, gate_cols(g)])
            z = _gelu(_dot(h, win16[:, gate_cols(g)]))
            v_scr[:, lin_cols(g)] = z
            ss = ss + jnp.sum(z * z, axis=-1, keepdims=True)
        r = lax.rsqrt(ss * (1.0 / dff) + EPS)

        row = lax.broadcasted_iota(jnp.int32, (A_CHUNK, A_CHUNK), 0) // CHUNK
        col = lax.broadcasted_iota(jnp.int32, (A_CHUNK, A_CHUNK), 1) // CHUNK
        causal = row >= col

        for g in range(n_groups):
            cols = lin_cols(g)
            if load_weights:
                w_in.take(n_groups + g, win16.at[:, cols])
                w_out.take(g, wout16.at[cols, :d])
            vn = ((v_scr[:, cols] * r) * gsgu_ref[:, cols]).astype(_BF16)
            w = jnp.where(causal, ws_ref[g], 0.0).astype(_BF16)
            bias = bs_ref[:, g:g + 1]
            sp = jnp.concatenate(
                [_dot(w, vn[k * A_CHUNK:(k + 1) * A_CHUNK]) + bias for k in range(tm // A_CHUNK)],
                axis=0)
            u = _gelu(_dot(h, win16[:, cols]))
            t_scr[:, cols] = (u * sp).astype(_BF16)

        o_ref[...] = x + _dot(t_scr[...], wout16[:, :d])

    first = pl.program_id(0) == 0
    pl.when(first)(functools.partial(body, True))
    pl.when(jnp.logical_not(first))(functools.partial(body, False))


def _gmlp_layer(x2, layer, g_norm, w_in, g_sgu, w_s, b_s_t, w_out):
    t, d = x2.shape
    n_groups = w_s.shape[1]
    dff = w_out.shape[1]
    gdim = dff // n_groups
    tm = GMLP_TILE
    kern = functools.partial(_gmlp_kernel, layer=layer, n_groups=n_groups, gdim=gdim)
    return pl.pallas_call(
        kern,
        grid=(t // tm,),
        in_specs=[
            pl.BlockSpec((tm, d), lambda i: (i, 0)),
            _layer_spec(g_norm.shape, layer),
            pl.BlockSpec(memory_space=pl.ANY),
            _layer_spec(g_sgu.shape, layer),
            _layer_spec(w_s.shape, layer),
            _layer_spec(b_s_t.shape, layer),
            pl.BlockSpec(memory_space=pl.ANY),
        ],
        out_specs=pl.BlockSpec((tm, d), lambda i: (i, 0)),
        out_shape=jax.ShapeDtypeStruct((t, d), _F32),
        scratch_shapes=[pltpu.VMEM((d, 2 * dff + PITCH_PAD), _BF16),
                        pltpu.VMEM((dff, d + PITCH_PAD), _BF16),
                        pltpu.VMEM((tm, dff), _F32), pltpu.VMEM((tm, dff), _BF16),
                        pltpu.VMEM((2 * STAGE_DEPTH, d, gdim), _F32),
                        pltpu.VMEM((STAGE_DEPTH, gdim, d), _F32),
                        pltpu.SemaphoreType.DMA((2 * STAGE_DEPTH,)),
                        pltpu.SemaphoreType.DMA((STAGE_DEPTH,))],
        compiler_params=pltpu.CompilerParams(
            dimension_semantics=("arbitrary",), vmem_limit_bytes=V7X_VMEM_LIMIT_BYTES),
        name="gmlp_mixer",
    )(x2, g_norm, w_in, g_sgu, w_s, b_s_t, w_out)


def _ffn_kernel(x_ref, gn_ref, wgu_hbm, wd_hbm, fn_ref, o_ref,
                wgu16, wd16, a_scr, gu_stage, d_stage, gu_sems, d_sems,
                *, layer, hidden, final_norm):
    ck = V7X_MXU_DIM
    n_chunks = hidden // ck

    def gate_cols(c):
        return slice(c * ck, (c + 1) * ck)

    def up_cols(c):
        return slice(hidden + c * ck, hidden + (c + 1) * ck)

    def body(load_weights):
        if load_weights:
            gu_chunks = [cols(c) for c in range(n_chunks) for cols in (gate_cols, up_cols)]
            w_gu = _WeightStream([wgu_hbm.at[layer, :, c] for c in gu_chunks], gu_stage, gu_sems)
            w_d = _WeightStream([wd_hbm.at[layer, gate_cols(c), :] for c in range(n_chunks)],
                                d_stage, d_sems)
            w_gu.prime()
            w_d.prime()

        x = x_ref[...]
        h = _rms_norm(x, gn_ref[...]).astype(_BF16)
        for c in range(n_chunks):
            if load_weights:
                w_gu.take(2 * c, wgu16.at[:, gate_cols(c)])
                w_gu.take(2 * c + 1, wgu16.at[:, up_cols(c)])
                w_d.take(c, wd16.at[gate_cols(c), :])
            gate = _dot(h, wgu16[:, gate_cols(c)])
            up = _dot(h, wgu16[:, up_cols(c)])
            a_scr[:, gate_cols(c)] = ((gate * jax.nn.sigmoid(gate)) * up).astype(_BF16)
        y = x + _dot(a_scr[...], wd16[...])
        if final_norm:
            y = _rms_norm(y, fn_ref[...])
        o_ref[...] = y

    first = pl.program_id(0) == 0
    pl.when(first)(functools.partial(body, True))
    pl.when(jnp.logical_not(first))(functools.partial(body, False))


def _ffn_layer(x2, layer, g_norm, w_gate_up, w_down, g_final, final_norm):
    t, d = x2.shape
    hidden = w_down.shape[1]
    ck = V7X_MXU_DIM
    assert hidden % ck == 0
    tm = FFN_TILE
    kern = functools.partial(_ffn_kernel, layer=layer, hidden=hidden, final_norm=final_norm)
    return pl.pallas_call(
        kern,
        grid=(t // tm,),
        in_specs=[
            pl.BlockSpec((tm, d), lambda i: (i, 0)),
            _layer_spec(g_norm.shape, layer),
            pl.BlockSpec(memory_space=pl.ANY),
            pl.BlockSpec(memory_space=pl.ANY),
            _const_spec((1, d)),
        ],
        out_specs=pl.BlockSpec((tm, d), lambda i: (i, 0)),
        out_shape=jax.ShapeDtypeStruct((t, d), _F32),
        scratch_shapes=[pltpu.VMEM((d, 2 * hidden), _BF16), pltpu.VMEM((hidden, d), _BF16),
                        pltpu.VMEM((tm, hidden), _BF16),
                        pltpu.VMEM((2 * STAGE_DEPTH, d, ck), _F32),
                        pltpu.VMEM((STAGE_DEPTH, ck, d), _F32),
                        pltpu.SemaphoreType.DMA((2 * STAGE_DEPTH,)),
                        pltpu.SemaphoreType.DMA((STAGE_DEPTH,))],
        compiler_params=pltpu.CompilerParams(
            dimension_semantics=("arbitrary",), vmem_limit_bytes=V7X_VMEM_LIMIT_BYTES),
        name="swiglu_ffn",
    )(x2, g_norm, w_gate_up, w_down, g_final)


def _kv_kernel(x_ref, gn_ref, w_ref, k_ref, v_ref):
    d = x_ref.shape[-1]
    h = _rms_norm(x_ref[...], gn_ref[...]).astype(_BF16)
    kv = _dot(h, w_ref[...])
    k_ref[...] = kv[:, :d].astype(_BF16)
    v_ref[...] = kv[:, d:].astype(_BF16)


def _shared_kv(x2, g_norm, w_kv):
    t, d = x2.shape
    tk = KV_TILE
    out = jax.ShapeDtypeStruct((t, d), _BF16)
    return pl.pallas_call(
        _kv_kernel,
        grid=(t // tk,),
        in_specs=[
            pl.BlockSpec((tk, d), lambda i: (i, 0)),
            _const_spec((1, d)),
            _const_spec((d, 2 * d)),
        ],
        out_specs=[pl.BlockSpec((tk, d), lambda i: (i, 0))] * 2,
        out_shape=[out, out],
        compiler_params=pltpu.CompilerParams(
            dimension_semantics=("arbitrary",), vmem_limit_bytes=V7X_VMEM_LIMIT_BYTES),
        name="shared_kv",
    )(x2, g_norm.reshape(1, d), w_kv.astype(_BF16))


def _bias_kernel(r_ref, o_ref):
    _, rows, width = o_ref.shape
    qc = lax.broadcasted_iota(jnp.int32, (1, width), 1) // CHUNK
    for r in range(rows):
        j = r // CHUNK
        in_band = (qc >= j) & (qc <= j + N_LEFT_CHUNKS)
        off = rows - 1 - r
        o_ref[:, r, :] = jnp.where(in_band, r_ref[:, off:off + width] * LOG2E, NEG_INF)


def _group_bias(rel_table, group_chunks):
    heads = rel_table.shape[0]
    rows = group_chunks * CHUNK
    width = (group_chunks + N_LEFT_CHUNKS) * CHUNK
    dist = (rows - 1 + N_LEFT_CHUNKS * CHUNK) - jnp.arange(rows - 1 + width)
    rev = rel_table[:, jnp.clip(dist, -MAX_REL, MAX_REL) + MAX_REL]
    return pl.pallas_call(
        _bias_kernel,
        out_shape=jax.ShapeDtypeStruct((heads, rows, width), _F32),
        name="rel_bias",
    )(rev)


def _attn_kernel(x_ref, gn_ref, wq_ref, kprev_ref, kcur_ref, vprev_ref, vcur_ref, bias_ref, wo_ref,
                 o_ref, q_scr, a_scr, k_scr, v_scr, s_scr,
                 *, heads, head_dim, group_chunks):
    tq = x_ref.shape[1]
    rows = group_chunks * CHUNK
    width = (group_chunks + N_LEFT_CHUNKS) * CHUNK
    pad = N_LEFT_CHUNKS * CHUNK
    pair = V7X_LANES // head_dim
    n_slots = s_scr.shape[0]
    i = pl.program_id(1)

    d = x_ref.shape[2]
    lane = lax.broadcasted_iota(jnp.int32, (1, V7X_LANES), 1)
    kcol = lax.broadcasted_iota(jnp.int32, (1, width), 1)

    n_groups = tq // rows
    n_slabs = heads // pair

    def lanes_of(slab):
        return slice(slab * V7X_LANES, (slab + 1) * V7X_LANES)

    def mine_of(sub):
        return (lane >= sub * head_dim) & (lane < (sub + 1) * head_dim)

    def run_groups(mask_pad):
        denom = {}
        probs = {}

        def scores(k):
            gi, slab = divmod(k, n_slabs)
            r0 = gi * rows
            q2 = q_scr[r0:r0 + rows, lanes_of(slab)]
            qm = jnp.concatenate(
                [jnp.where(mine_of(sub), q2, jnp.zeros_like(q2)) for sub in range(pair)], axis=0)
            k2 = k_scr[r0:r0 + width, lanes_of(slab)]
            s = lax.dot_general(qm, k2, (((1,), (1,)), ((), ())),
                                preferred_element_type=_F32)
            s = s + bias_ref[slab]
            if mask_pad:
                valid = (i * tq + r0 + kcol) >= pad
                s = jnp.where(valid, s, NEG_INF)
            s_scr[k % n_slots] = s

        def softmax(k):
            s = s_scr[k % n_slots]
            p = jnp.exp2(s - jnp.max(s, axis=-1, keepdims=True))
            denom[k] = jnp.sum(p, axis=-1, keepdims=True)
            probs[k] = p.astype(_BF16)

        def weighted_values(k):
            gi, slab = divmod(k, n_slabs)
            r0 = gi * rows
            v2 = v_scr[r0:r0 + width, lanes_of(slab)]
            o = _dot(probs.pop(k), v2) * (1.0 / denom.pop(k))
            out = o[:rows]
            for sub in range(1, pair):
                out = jnp.where(mine_of(sub), o[sub * rows:(sub + 1) * rows], out)
            a_scr[r0:r0 + rows, lanes_of(slab)] = out.astype(_BF16)

        n_items = n_groups * n_slabs
        for t in range(n_items + PV_LAG):
            if t < n_items:
                scores(t)
            if SOFTMAX_LAG <= t < n_items + SOFTMAX_LAG:
                softmax(t - SOFTMAX_LAG)
            if t >= PV_LAG:
                weighted_values(t - PV_LAG)

    def tile_body(mask_pad):
        k_scr[:pad, :d] = kprev_ref[0]
        k_scr[pad:, :d] = kcur_ref[0]
        v_scr[:pad, :d] = vprev_ref[0]
        v_scr[pad:, :d] = vcur_ref[0]

        x = x_ref[0]
        h = _rms_norm(x, gn_ref[...]).astype(_BF16)
        q_scr[...] = (_dot(h, wq_ref[...]) * (head_dim ** -0.5 * LOG2E)).astype(_BF16)
        run_groups(mask_pad)
        o_ref[0] = x + _dot(a_scr[...], wo_ref[...])

    pl.when(i * tq < pad)(functools.partial(tile_body, True))
    pl.when(i * tq >= pad)(functools.partial(tile_body, False))


def _attn_layer(x3, layer, g_norm, w_q, bias, w_o, k3, v3, heads):
    b, s, d = x3.shape
    head_dim = d // heads
    tq = ATTN_TILE
    rows = Q_GROUP_CHUNKS * CHUNK
    width = (Q_GROUP_CHUNKS + N_LEFT_CHUNKS) * CHUNK
    pad = N_LEFT_CHUNKS * CHUNK
    pair = V7X_LANES // head_dim
    assert tq % pad == 0
    bias = bias.reshape(heads // pair, pair * rows, width)
    kern = functools.partial(_attn_kernel, heads=heads, head_dim=head_dim,
                             group_chunks=Q_GROUP_CHUNKS)
    lo_spec = pl.BlockSpec((1, pad, d), lambda bi, i: (bi, jnp.maximum(i * (tq // pad) - 1, 0), 0))
    hi_spec = pl.BlockSpec((1, tq, d), lambda bi, i: (bi, i, 0))
    return pl.pallas_call(
        kern,
        grid=(b, s // tq),
        in_specs=[
            pl.BlockSpec((1, tq, d), lambda bi, i: (bi, i, 0)),
            _layer_spec(g_norm.shape, layer),
            _layer_spec(w_q.shape, layer),
            lo_spec, hi_spec, lo_spec, hi_spec,
            _const_spec(bias.shape),
            _layer_spec(w_o.shape, layer),
        ],
        out_specs=pl.BlockSpec((1, tq, d), lambda bi, i: (bi, i, 0)),
        out_shape=jax.ShapeDtypeStruct((b, s, d), _F32),
        scratch_shapes=[pltpu.VMEM((tq, d), _BF16), pltpu.VMEM((tq, d), _BF16),
                        pltpu.VMEM((pad + tq, d + PITCH_PAD), _BF16),
                        pltpu.VMEM((pad + tq, d + PITCH_PAD), _BF16),
                        pltpu.VMEM((ATTN_SLOTS, pair * rows, width), _F32)],
        compiler_params=pltpu.CompilerParams(
            dimension_semantics=("arbitrary", "arbitrary"),
            vmem_limit_bytes=V7X_VMEM_LIMIT_BYTES),
        name="banded_attention",
    )(x3, g_norm, w_q, k3, k3, v3, v3, bias, w_o)


def kernel(x, a_norm, a_w_in, a_sgu_norm, a_w_spatial, a_b_spatial, a_w_out, kv_norm, w_kv,
           b_norm, b_w_q, b_rel_bias, b_w_o, ffn_norm, ffn_w_gate_up, ffn_w_down, final_norm):
    b, s, d = x.shape
    n_a = a_norm.shape[0]
    depth = ffn_norm.shape[0]
    heads = b_rel_bias.shape[1]
    assert s % ATTN_TILE == 0 and (b * s) % max(GMLP_TILE, FFN_TILE, KV_TILE) == 0
    assert ATTN_TILE % (Q_GROUP_CHUNKS * CHUNK) == 0 and GMLP_TILE % A_CHUNK == 0

    a_norm3 = a_norm[:, None, :]
    a_sgu3 = a_sgu_norm[:, None, :]
    a_bs_t = jnp.swapaxes(a_b_spatial, 1, 2)
    b_norm3 = b_norm[:, None, :]
    b_w_q16, b_w_o16 = b_w_q.astype(_BF16), b_w_o.astype(_BF16)
    ffn_norm3 = ffn_norm[:, None, :]
    g_final = final_norm.reshape(1, d)

    x2 = x.reshape(b * s, d)
    k3 = v3 = None
    for layer in range(depth):
        if layer < n_a:
            x2 = _gmlp_layer(x2, layer, a_norm3, a_w_in, a_sgu3, a_w_spatial, a_bs_t, a_w_out)
        else:
            x3 = x2.reshape(b, s, d)
            if layer == n_a:
                k2, v2 = _shared_kv(x2, kv_norm, w_kv)
                k3, v3 = k2.reshape(b, s, d), v2.reshape(b, s, d)
            li = layer - n_a
            bias = _group_bias(b_rel_bias[li], Q_GROUP_CHUNKS)
            x2 = _attn_layer(x3, li, b_norm3, b_w_q16, bias, b_w_o16, k3, v3,
                             heads).reshape(b * s, d)
        x2 = _ffn_layer(x2, layer, ffn_norm3, ffn_w_gate_up, ffn_w_down, g_final,
                        final_norm=(layer == depth - 1))
    return x2.reshape(b, s, d)
```

```python
import functools
import math

import jax
import jax.numpy as jnp
from jax import lax
from jax.experimental import pallas as pl
from jax.experimental.pallas import tpu as pltpu

EPS = 1e-6
CHUNK = 64
A_CHUNK = 128
N_LEFT_CHUNKS = 8
MAX_REL = 256
NEG_INF = -1e30
LOG2E = math.log2(math.e)

V7X_LANES = 128
V7X_MXU_DIM = 256
V7X_VMEM_LIMIT_BYTES = 56 * 1024 * 1024

GMLP_TILE = 512
GMLP_U_LEAD = 2
FFN_TILE = 512
KV_TILE = 1024
SOFTMAX_LAG = 1
PV_LAG = 4
Q_GROUP_CHUNKS = 2
ATTN_TILE = 512
ATTN_SLOTS = 4
STAGE_DEPTH = 3
PITCH_PAD = V7X_LANES

_BF16 = jnp.bfloat16
_F32 = jnp.float32


def _const_spec(shape):
    nd = len(shape)
    return pl.BlockSpec(shape, lambda *_: (0,) * nd, pipeline_mode=pl.Buffered(1))


def _layer_spec(stacked_shape, layer):
    nd = len(stacked_shape)
    return pl.BlockSpec((None,) + tuple(stacked_shape[1:]),
                        lambda *_: (layer,) + (0,) * (nd - 1), pipeline_mode=pl.Buffered(1))


def _rms_norm(x, g):
    return (x * lax.rsqrt(jnp.mean(x * x, axis=-1, keepdims=True) + EPS)) * g


def _dot(a, b):
    return jnp.dot(a, b, preferred_element_type=_F32)


def _gelu(x):
    return 0.5 * x * (1.0 + lax.erf(x * (1.0 / math.sqrt(2.0))))


class _WeightStream:
    def __init__(self, sources, stage, sems):
        self.sources, self.stage, self.sems = sources, stage, sems
        self.depth = stage.shape[0]

    def _copy(self, k):
        slot = k % self.depth
        return pltpu.make_async_copy(self.sources[k], self.stage.at[slot], self.sems.at[slot])

    def prime(self):
        for k in range(min(self.depth, len(self.sources))):
            self._copy(k).start()

    def take(self, k, dst):
        self._copy(k).wait()
        dst[...] = self.stage[k % self.depth].astype(_BF16)
        if k + self.depth < len(self.sources):
            self._copy(k + self.depth).start()


def _gmlp_kernel(x_ref, gn_ref, win_hbm, gsgu_ref, ws_ref, bs_ref, wout_hbm, o_ref,
                 win16, wout16, v_scr, t_scr, in_stage, out_stage, in_sems, out_sems,
                 *, layer, n_groups, gdim):
    tm, d = x_ref.shape
    dff = n_groups * gdim

    def gate_cols(g):
        return slice(dff + g * gdim, dff + (g + 1) * gdim)

    def lin_cols(g):
        return slice(g * gdim, (g + 1) * gdim)

    def body(load_weights):
        if load_weights:
            in_chunks = [gate_cols(g) for g in range(n_groups)] + [lin_cols(g) for g in range(n_groups)]
            w_in = _WeightStream([win_hbm.at[layer, :, c] for c in in_chunks], in_stage, in_sems)
            w_out = _WeightStream([wout_hbm.at[layer, lin_cols(g), :] for g in range(n_groups)],
                                  out_stage, out_sems)
            w_in.prime()
            w_out.prime()

        x = x_ref[...]
        h = _rms_norm(x, gn_ref[...]).astype(_BF16)

        ss = jnp.zeros((tm, 1), _F32)
        for g in range(n_groups):
            if load_weights:
                w_in.take(g, win16.at[:, gate_cols(g)])
            z = _gelu(_dot(h, win16[:, gate_cols(g)]))
            v_scr[:, lin_cols(g)] = z
            ss = ss + jnp.sum(z * z, axis=-1, keepdims=True)
        r = lax.rsqrt(ss * (1.0 / dff) + EPS)

        row = lax.broadcasted_iota(jnp.int32, (A_CHUNK, A_CHUNK), 0) // CHUNK
        col = lax.broadcasted_iota(jnp.int32, (A_CHUNK, A_CHUNK), 1) // CHUNK
        causal = row >= col

        lin = {}
        for step in range(n_groups + GMLP_U_LEAD):
            if step < n_groups:
                if load_weights:
                    w_in.take(n_groups + step, win16.at[:, lin_cols(step)])
                    w_out.take(step, wout16.at[lin_cols(step), :d])
                lin[step] = _gelu(_dot(h, win16[:, lin_cols(step)]))
            g = step - GMLP_U_LEAD
            if g >= 0:
                cols = lin_cols(g)
                vn = ((v_scr[:, cols] * r) * gsgu_ref[:, cols]).astype(_BF16)
                w = jnp.where(causal, ws_ref[g], 0.0).astype(_BF16)
                bias = bs_ref[:, g:g + 1]
                sp = jnp.concatenate(
                    [_dot(w, vn[k * A_CHUNK:(k + 1) * A_CHUNK]) + bias for k in range(tm // A_CHUNK)],
                    axis=0)
                t_scr[:, cols] = (lin.pop(g) * sp).astype(_BF16)

        o_ref[...] = x + _dot(t_scr[...], wout16[:, :d])

    first = pl.program_id(0) == 0
    pl.when(first)(functools.partial(body, True))
    pl.when(jnp.logical_not(first))(functools.partial(body, False))


def _gmlp_layer(x2, layer, g_norm, w_in, g_sgu, w_s, b_s_t, w_out):
    t, d = x2.shape
    n_groups = w_s.shape[1]
    dff = w_out.shape[1]
    gdim = dff // n_groups
    tm = GMLP_TILE
    kern = functools.partial(_gmlp_kernel, layer=layer, n_groups=n_groups, gdim=gdim)
    return pl.pallas_call(
        kern,
        grid=(t // tm,),
        in_specs=[
            pl.BlockSpec((tm, d), lambda i: (i, 0)),
            _layer_spec(g_norm.shape, layer),
            pl.BlockSpec(memory_space=pl.ANY),
            _layer_spec(g_sgu.shape, layer),
            _layer_spec(w_s.shape, layer),
            _layer_spec(b_s_t.shape, layer),
            pl.BlockSpec(memory_space=pl.ANY),
        ],
        out_specs=pl.BlockSpec((tm, d), lambda i: (i, 0)),
        out_shape=jax.ShapeDtypeStruct((t, d), _F32),
        scratch_shapes=[pltpu.VMEM((d, 2 * dff + PITCH_PAD), _BF16),
                        pltpu.VMEM((dff, d + PITCH_PAD), _BF16),
                        pltpu.VMEM((tm, dff), _F32), pltpu.VMEM((tm, dff), _BF16),
                        pltpu.VMEM((2 * STAGE_DEPTH, d, gdim), _F32),
                        pltpu.VMEM((STAGE_DEPTH, gdim, d), _F32),
                        pltpu.SemaphoreType.DMA((2 * STAGE_DEPTH,)),
                        pltpu.SemaphoreType.DMA((STAGE_DEPTH,))],
        compiler_params=pltpu.CompilerParams(
            dimension_semantics=("arbitrary",), vmem_limit_bytes=V7X_VMEM_LIMIT_BYTES),
        name="gmlp_mixer",
    )(x2, g_norm, w_in, g_sgu, w_s, b_s_t, w_out)


def _ffn_kernel(x_ref, gn_ref, wgu_hbm, wd_hbm, fn_ref, o_ref,
                wgu16, wd16, a_scr, gu_stage, d_stage, gu_sems, d_sems,
                *, layer, hidden, final_norm):
    ck = V7X_MXU_DIM
    n_chunks = hidden // ck

    def gate_cols(c):
        return slice(c * ck, (c + 1) * ck)

    def up_cols(c):
        return slice(hidden + c * ck, hidden + (c + 1) * ck)

    def body(load_weights):
        if load_weights:
            gu_chunks = [cols(c) for c in range(n_chunks) for cols in (gate_cols, up_cols)]
            w_gu = _WeightStream([wgu_hbm.at[layer, :, c] for c in gu_chunks], gu_stage, gu_sems)
            w_d = _WeightStream([wd_hbm.at[layer, gate_cols(c), :] for c in range(n_chunks)],
                                d_stage, d_sems)
            w_gu.prime()
            w_d.prime()

        x = x_ref[...]
        h = _rms_norm(x, gn_ref[...]).astype(_BF16)
        for c in range(n_chunks):
            if load_weights:
                w_gu.take(2 * c, wgu16.at[:, gate_cols(c)])
                w_gu.take(2 * c + 1, wgu16.at[:, up_cols(c)])
                w_d.take(c, wd16.at[gate_cols(c), :])
            gate = _dot(h, wgu16[:, gate_cols(c)])
            up = _dot(h, wgu16[:, up_cols(c)])
            a_scr[:, gate_cols(c)] = ((gate * jax.nn.sigmoid(gate)) * up).astype(_BF16)
        y = x + _dot(a_scr[...], wd16[...])
        if final_norm:
            y = _rms_norm(y, fn_ref[...])
        o_ref[...] = y

    first = pl.program_id(0) == 0
    pl.when(first)(functools.partial(body, True))
    pl.when(jnp.logical_not(first))(functools.partial(body, False))


def _ffn_layer(x2, layer, g_norm, w_gate_up, w_down, g_final, final_norm):
    t, d = x2.shape
    hidden = w_down.shape[1]
    ck = V7X_MXU_DIM
    assert hidden % ck == 0
    tm = FFN_TILE
    kern = functools.partial(_ffn_kernel, layer=layer, hidden=hidden, final_norm=final_norm)
    return pl.pallas_call(
        kern,
        grid=(t // tm,),
        in_specs=[
            pl.BlockSpec((tm, d), lambda i: (i, 0)),
            _layer_spec(g_norm.shape, layer),
            pl.BlockSpec(memory_space=pl.ANY),
            pl.BlockSpec(memory_space=pl.ANY),
            _const_spec((1, d)),
        ],
        out_specs=pl.BlockSpec((tm, d), lambda i: (i, 0)),
        out_shape=jax.ShapeDtypeStruct((t, d), _F32),
        scratch_shapes=[pltpu.VMEM((d, 2 * hidden), _BF16), pltpu.VMEM((hidden, d), _BF16),
                        pltpu.VMEM((tm, hidden), _BF16),
                        pltpu.VMEM((2 * STAGE_DEPTH, d, ck), _F32),
                        pltpu.VMEM((STAGE_DEPTH, ck, d), _F32),
                        pltpu.SemaphoreType.DMA((2 * STAGE_DEPTH,)),
                        pltpu.SemaphoreType.DMA((STAGE_DEPTH,))],
        compiler_params=pltpu.CompilerParams(
            dimension_semantics=("arbitrary",), vmem_limit_bytes=V7X_VMEM_LIMIT_BYTES),
        name="swiglu_ffn",
    )(x2, g_norm, w_gate_up, w_down, g_final)


def _kv_kernel(x_ref, gn_ref, w_ref, k_ref, v_ref):
    d = x_ref.shape[-1]
    h = _rms_norm(x_ref[...], gn_ref[...]).astype(_BF16)
    kv = _dot(h, w_ref[...])
    k_ref[...] = kv[:, :d].astype(_BF16)
    v_ref[...] = kv[:, d:].astype(_BF16)


def _shared_kv(x2, g_norm, w_kv):
    t, d = x2.shape
    tk = KV_TILE
    out = jax.ShapeDtypeStruct((t, d), _BF16)
    return pl.pallas_call(
        _kv_kernel,
        grid=(t // tk,),
        in_specs=[
            pl.BlockSpec((tk, d), lambda i: (i, 0)),
            _const_spec((1, d)),
            _const_spec((d, 2 * d)),
        ],
        out_specs=[pl.BlockSpec((tk, d), lambda i: (i, 0))] * 2,
        out_shape=[out, out],
        compiler_params=pltpu.CompilerParams(
            dimension_semantics=("arbitrary",), vmem_limit_bytes=V7X_VMEM_LIMIT_BYTES),
        name="shared_kv",
    )(x2, g_norm.reshape(1, d), w_kv.astype(_BF16))


def _bias_kernel(r_ref, o_ref):
    _, rows, width = o_ref.shape
    qc = lax.broadcasted_iota(jnp.int32, (1, width), 1) // CHUNK
    for r in range(rows):
        j = r // CHUNK
        in_band = (qc >= j) & (qc <= j + N_LEFT_CHUNKS)
        off = rows - 1 - r
        o_ref[:, r, :] = jnp.where(in_band, r_ref[:, off:off + width] * LOG2E, NEG_INF)


def _group_bias(rel_table, group_chunks):
    heads = rel_table.shape[0]
    rows = group_chunks * CHUNK
    width = (group_chunks + N_LEFT_CHUNKS) * CHUNK
    dist = (rows - 1 + N_LEFT_CHUNKS * CHUNK) - jnp.arange(rows - 1 + width)
    rev = rel_table[:, jnp.clip(dist, -MAX_REL, MAX_REL) + MAX_REL]
    return pl.pallas_call(
        _bias_kernel,
        out_shape=jax.ShapeDtypeStruct((heads, rows, width), _F32),
        name="rel_bias",
    )(rev)


def _attn_kernel(x_ref, gn_ref, wq_ref, kprev_ref, kcur_ref, vprev_ref, vcur_ref, bias_ref, wo_ref,
                 o_ref, q_scr, a_scr, k_scr, v_scr, s_scr,
                 *, heads, head_dim, group_chunks):
    tq = x_ref.shape[1]
    rows = group_chunks * CHUNK
    width = (group_chunks + N_LEFT_CHUNKS) * CHUNK
    pad = N_LEFT_CHUNKS * CHUNK
    pair = V7X_LANES // head_dim
    n_slots = s_scr.shape[0]
    i = pl.program_id(1)

    d = x_ref.shape[2]
    lane = lax.broadcasted_iota(jnp.int32, (1, V7X_LANES), 1)
    kcol = lax.broadcasted_iota(jnp.int32, (1, width), 1)

    n_groups = tq // rows
    n_slabs = heads // pair

    def lanes_of(slab):
        return slice(slab * V7X_LANES, (slab + 1) * V7X_LANES)

    def mine_of(sub):
        return (lane >= sub * head_dim) & (lane < (sub + 1) * head_dim)

    def run_groups(mask_pad):
        denom = {}
        probs = {}

        def scores(k):
            gi, slab = divmod(k, n_slabs)
            r0 = gi * rows
            q2 = q_scr[r0:r0 + rows, lanes_of(slab)]
            qm = jnp.concatenate(
                [jnp.where(mine_of(sub), q2, jnp.zeros_like(q2)) for sub in range(pair)], axis=0)
            k2 = k_scr[r0:r0 + width, lanes_of(slab)]
            s = lax.dot_general(qm, k2, (((1,), (1,)), ((), ())),
                                preferred_element_type=_F32)
            s = s + bias_ref[slab]
            if mask_pad:
                valid = (i * tq + r0 + kcol) >= pad
                s = jnp.where(valid, s, NEG_INF)
            s_scr[k % n_slots] = s

        def softmax(k):
            s = s_scr[k % n_slots]
            p = jnp.exp2(s - jnp.max(s, axis=-1, keepdims=True))
            denom[k] = jnp.sum(p, axis=-1, keepdims=True)
            probs[k] = p.astype(_BF16)

        def weighted_values(k):
            gi, slab = divmod(k, n_slabs)
            r0 = gi * rows
            v2 = v_scr[r0:r0 + width, lanes_of(slab)]
            o = _dot(probs.pop(k), v2) * (1.0 / denom.pop(k))
            out = o[:rows]
            for sub in range(1, pair):
                out = jnp.where(mine_of(sub), o[sub * rows:(sub + 1) * rows], out)
            a_scr[r0:r0 + rows, lanes_of(slab)] = out.astype(_BF16)

        n_items = n_groups * n_slabs
        for t in range(n_items + PV_LAG):
            if t < n_items:
                scores(t)
            if SOFTMAX_LAG <= t < n_items + SOFTMAX_LAG:
                softmax(t - SOFTMAX_LAG)
            if t >= PV_LAG:
                weighted_values(t - PV_LAG)

    def tile_body(mask_pad):
        k_scr[:pad, :d] = kprev_ref[0]
        k_scr[pad:, :d] = kcur_ref[0]
        v_scr[:pad, :d] = vprev_ref[0]
        v_scr[pad:, :d] = vcur_ref[0]

        x = x_ref[0]
        h = _rms_norm(x, gn_ref[...]).astype(_BF16)
        q_scr[...] = (_dot(h, wq_ref[...]) * (head_dim ** -0.5 * LOG2E)).astype(_BF16)
        run_groups(mask_pad)
        o_ref[0] = x + _dot(a_scr[...], wo_ref[...])

    pl.when(i * tq < pad)(functools.partial(tile_body, True))
    pl.when(i * tq >= pad)(functools.partial(tile_body, False))


def _attn_layer(x3, layer, g_norm, w_q, bias, w_o, k3, v3, heads):
    b, s, d = x3.shape
    head_dim = d // heads
    tq = ATTN_TILE
    rows = Q_GROUP_CHUNKS * CHUNK
    width = (Q_GROUP_CHUNKS + N_LEFT_CHUNKS) * CHUNK
    pad = N_LEFT_CHUNKS * CHUNK
    pair = V7X_LANES // head_dim
    assert tq % pad == 0
    bias = bias.reshape(heads // pair, pair * rows, width)
    kern = functools.partial(_attn_kernel, heads=heads, head_dim=head_dim,
                             group_chunks=Q_GROUP_CHUNKS)
    lo_spec = pl.BlockSpec((1, pad, d), lambda bi, i: (bi, jnp.maximum(i * (tq // pad) - 1, 0), 0))
    hi_spec = pl.BlockSpec((1, tq, d), lambda bi, i: (bi, i, 0))
    return pl.pallas_call(
        kern,
        grid=(b, s // tq),
        in_specs=[
            pl.BlockSpec((1, tq, d), lambda bi, i: (bi, i, 0)),
            _layer_spec(g_norm.shape, layer),
            _layer_spec(w_q.shape, layer),
            lo_spec, hi_spec, lo_spec, hi_spec,
            _const_spec(bias.shape),
            _layer_spec(w_o.shape, layer),
        ],
        out_specs=pl.BlockSpec((1, tq, d), lambda bi, i: (bi, i, 0)),
        out_shape=jax.ShapeDtypeStruct((b, s, d), _F32),
        scratch_shapes=[pltpu.VMEM((tq, d), _BF16), pltpu.VMEM((tq, d), _BF16),
                        pltpu.VMEM((pad + tq, d + PITCH_PAD), _BF16),
                        pltpu.VMEM((pad + tq, d + PITCH_PAD), _BF16),
                        pltpu.VMEM((ATTN_SLOTS, pair * rows, width), _F32)],
        compiler_params=pltpu.CompilerParams(
            dimension_semantics=("arbitrary", "arbitrary"),
            vmem_limit_bytes=V7X_VMEM_LIMIT_BYTES),
        name="banded_attention",
    )(x3, g_norm, w_q, k3, k3, v3, v3, bias, w_o)


def kernel(x, a_norm, a_w_in, a_sgu_norm, a_w_spatial, a_b_spatial, a_w_out, kv_norm, w_kv,
           b_norm, b_w_q, b_rel_bias, b_w_o, ffn_norm, ffn_w_gate_up, ffn_w_down, final_norm):
    b, s, d = x.shape
    n_a = a_norm.shape[0]
    depth = ffn_norm.shape[0]
    heads = b_rel_bias.shape[1]
    assert s % ATTN_TILE == 0 and (b * s) % max(GMLP_TILE, FFN_TILE, KV_TILE) == 0
    assert ATTN_TILE % (Q_GROUP_CHUNKS * CHUNK) == 0 and GMLP_TILE % A_CHUNK == 0

    a_norm3 = a_norm[:, None, :]
    a_sgu3 = a_sgu_norm[:, None, :]
    a_bs_t = jnp.swapaxes(a_b_spatial, 1, 2)
    b_norm3 = b_norm[:, None, :]
    b_w_q16, b_w_o16 = b_w_q.astype(_BF16), b_w_o.astype(_BF16)
    ffn_norm3 = ffn_norm[:, None, :]
    g_final = final_norm.reshape(1, d)

    x2 = x.reshape(b * s, d)
    k3 = v3 = None
    for layer in range(depth):
        if layer < n_a:
            x2 = _gmlp_layer(x2, layer, a_norm3, a_w_in, a_sgu3, a_w_spatial, a_bs_t, a_w_out)
        else:
            x3 = x2.reshape(b, s, d)
            if layer == n_a:
                k2, v2 = _shared_kv(x2, kv_norm, w_kv)
                k3, v3 = k2.reshape(b, s, d), v2.reshape(b, s, d)
            li = layer - n_a
            bias = _group_bias(b_rel_bias[li], Q_GROUP_CHUNKS)
            x2 = _attn_layer(x3, li, b_norm3, b_w_q16, bias, b_w_o16, k3, v3,
                             heads).reshape(b * s, d)
        x2 = _ffn_layer(x2, layer, ffn_norm3, ffn_w_gate_up, ffn_w_down, g_final,
                        final_norm=(layer == depth - 1))
    return x2.reshape(b, s, d)
```
